```python
import math
import jax
import jax.numpy as jnp
from jax import lax
import numpy as np

D_MODEL = 2048
BATCH = 4
SEQ = 2048
DEPTH = 2
DEC_BATCH = 8
DEC_SEQ = 64
PAST_LEN = 1024

CHUNK = 64
QBLK = 128
ROPE_THETA = 500000.0
EPS = 1e-6
N_BRANCH = 3
H_A = 8
DH_A = 64
VA = 2 * DH_A
ROT_A = DH_A // 4
H_B = 8
Q_LORA = 512
KV_LORA = 256
NOPE_B = 64
ROPE_B = 32
V_B = 128
H_C = 8
DK_C = 128
DV_C = 128
CONV_W = 4
QKV_C = 2 * H_C * DK_C + H_C * DV_C
BRANCH_W = H_A * VA
N_GROUPS = 4
EXPERTS_PER_GROUP = 4
N_EXPERTS = N_GROUPS * EXPERTS_PER_GROUP
TOP_K = 2
D_EXPERT = 512
IN_SIZES = (H_A * 2 * DH_A, H_A * 2 * DH_A, H_A * VA, Q_LORA, KV_LORA, ROPE_B,
            QKV_C, H_C * DV_C, H_C, H_C, N_BRANCH * D_MODEL)
IN_COLS = sum(IN_SIZES)

kernel_name = 'hybrid_streaming_encoder_step'


def rmsnorm(x, w):
    xf = x.astype(jnp.float32)
    y = xf * lax.rsqrt(jnp.mean(xf * xf, axis=-1, keepdims=True) + EPS)
    return (y * w.astype(jnp.float32)).astype(x.dtype)


def l2norm(x):
    return x * lax.rsqrt(jnp.sum(x * x, axis=-1, keepdims=True) + EPS)


def apply_rope(x, pos, rot_dim):
    half = rot_dim // 2
    inv = ROPE_THETA ** (-jnp.arange(half, dtype=jnp.float32) * 2.0 / rot_dim)
    ang = pos.astype(jnp.float32)[:, None] * inv[None, :]
    shape = (1, pos.shape[0]) + (1,) * (x.ndim - 3) + (half,)
    cos = jnp.cos(ang).reshape(shape)
    sin = jnp.sin(ang).reshape(shape)
    xr = x[..., :rot_dim].astype(jnp.float32)
    x1, x2 = xr[..., :half], xr[..., half:]
    rot = jnp.concatenate([x1 * cos - x2 * sin, x2 * cos + x1 * sin], axis=-1).astype(x.dtype)
    return jnp.concatenate([rot, x[..., rot_dim:]], axis=-1)


def chunk_mask(q_pos, k_pos):
    return (q_pos // CHUNK)[:, None] >= (k_pos // CHUNK)[None, :]


def over_query_blocks(core, q, q_pos):
    B, T = q.shape[:2]
    if T % QBLK != 0:
        return core(q, q_pos)
    nb = T // QBLK
    qb = jnp.moveaxis(q.reshape((B, nb, QBLK) + q.shape[2:]), 1, 0)
    pb = q_pos.reshape(nb, QBLK)
    out = lax.map(lambda a: core(a[0], a[1]), (qb, pb))
    return jnp.moveaxis(out, 0, 1).reshape((B, T) + out.shape[3:])


def diff_attention(q, k, v, q_pos, k_pos, lam):
    scale = DH_A ** -0.5

    def core(qb, pb):
        s = jnp.einsum('bqhmd,bkhmd->bhmqk', qb, k).astype(jnp.float32) * scale
        s = jnp.where(chunk_mask(pb, k_pos), s, -jnp.inf)
        p = jax.nn.softmax(s, axis=-1)
        pd = p[:, :, 0] - lam * p[:, :, 1]
        return jnp.einsum('bhqk,bkhv->bqhv', pd.astype(v.dtype), v)

    return over_query_blocks(core, q, q_pos)


def softmax_attention(q, k, v, q_pos, k_pos):
    scale = q.shape[-1] ** -0.5

    def core(qb, pb):
        s = jnp.einsum('bqhd,bkhd->bhqk', qb, k).astype(jnp.float32) * scale
        s = jnp.where(chunk_mask(pb, k_pos), s, -jnp.inf)
        p = jax.nn.softmax(s, axis=-1)
        return jnp.einsum('bhqk,bkhv->bqhv', p.astype(v.dtype), v)

    return over_query_blocks(core, q, q_pos)


def gated_delta_rule(q, k, v, beta, g, s0):
    B, T, H, DK = q.shape
    DV = v.shape[-1]
    L = CHUNK if T % CHUNK == 0 else T
    N = T // L

    def blk(a):
        return jnp.swapaxes(a.reshape((B, N, L, H) + a.shape[3:]), 2, 3)

    qc, kc, vc, bc, gc = blk(q), blk(k), blk(v), blk(beta), blk(g)
    gcum = jnp.cumsum(gc, axis=-1)
    idx = jnp.arange(L)
    incl = idx[:, None] >= idx[None, :]
    strict = idx[:, None] > idx[None, :]
    seg = gcum[..., :, None] - gcum[..., None, :]
    decay = jnp.where(incl, jnp.exp(jnp.where(incl, seg, 0.0)), 0.0)
    kk = jnp.einsum('bnhid,bnhjd->bnhij', kc, kc)
    a_mat = jnp.where(strict, bc[..., :, None] * kk * decay, 0.0)
    m = a_mat + jnp.eye(L, dtype=jnp.float32)
    gam = jnp.exp(gcum)
    rhs = jnp.concatenate([vc * bc[..., None], kc * (bc * gam)[..., None]], axis=-1)
    sol = lax.linalg.triangular_solve(m, rhs, left_side=True, lower=True, unit_diagonal=True)
    u0, w = sol[..., :DV], sol[..., DV:]
    pq = jnp.einsum('bnhid,bnhjd->bnhij', qc, kc) * decay
    gam_end = gam[..., -1]
    to_end = jnp.exp(gcum[..., -1:] - gcum)
    xs = tuple(jnp.moveaxis(a, 1, 0) for a in (u0, w, pq, qc, kc, gam, gam_end, to_end))

    def step(s, xc):
        u0_c, w_c, pq_c, q_c, k_c, gam_c, gend_c, tend_c = xc
        u = u0_c - jnp.einsum('bhlk,bhvk->bhlv', w_c, s)
        o = gam_c[..., None] * jnp.einsum('bhlk,bhvk->bhlv', q_c, s) + jnp.einsum('bhij,bhjv->bhiv', pq_c, u)
        s = gend_c[..., None, None] * s + jnp.einsum('bhlv,bhlk->bhvk', u * tend_c[..., None], k_c)
        return s, o

    s_fin, o = lax.scan(step, s0, xs)
    o = jnp.swapaxes(jnp.moveaxis(o, 0, 1), 2, 3).reshape(B, T, H, DV)
    return o, s_fin


def hier_moe(h, w_rg, b_rg, w_re, b_re, w_gate, w_up, w_down):
    B, T, D = h.shape
    t = h.reshape(B * T, D)
    gp = jax.nn.softmax((t @ w_rg).astype(jnp.float32) + b_rg.astype(jnp.float32), axis=-1)
    grp = jnp.argmax(gp, axis=-1)
    pg = jnp.take_along_axis(gp, grp[:, None], axis=-1)
    el = ((t @ w_re).astype(jnp.float32) + b_re.astype(jnp.float32)).reshape(-1, N_GROUPS, EXPERTS_PER_GROUP)
    el_sel = jnp.take_along_axis(el, grp[:, None, None], axis=1)[:, 0]
    ep = jax.nn.softmax(el_sel, axis=-1)
    top_v, top_i = lax.top_k(ep, TOP_K)
    wts = pg * top_v / jnp.sum(top_v, axis=-1, keepdims=True)
    eid = grp[:, None] * EXPERTS_PER_GROUP + top_i
    combine = jnp.sum(jax.nn.one_hot(eid, N_EXPERTS, dtype=jnp.float32) * wts[..., None], axis=1)
    hid = jax.nn.silu(jnp.einsum('td,edf->tef', t, w_gate)) * jnp.einsum('td,edf->tef', t, w_up)
    y = jnp.einsum('tef,efd->td', hid * combine[..., None].astype(hid.dtype), w_down)
    return y.reshape(B, T, D)


def setup_inputs(seed: int = 0) -> dict:
    key = jax.random.key(seed)
    ks = jax.random.split(key, 32)

    def nrm(i, shape, scale):
        return jax.random.normal(ks[i], shape, jnp.float32) * scale

    def gain(i, shape):
        return 1.0 + nrm(i, shape, 0.02)

    dt = jnp.exp(jax.random.uniform(ks[17], (DEPTH, H_C), jnp.float32, math.log(1e-3), math.log(1e-1)))
    return {
        'x_prompt': nrm(0, (BATCH, SEQ, D_MODEL), 1.0),
        'x_sample': nrm(1, (DEC_BATCH, DEC_SEQ, D_MODEL), 1.0),
        'cache_diff_k': nrm(2, (DEPTH, DEC_BATCH, PAST_LEN, H_A, 2, DH_A), 1.0),
        'cache_diff_v': nrm(3, (DEPTH, DEC_BATCH, PAST_LEN, H_A, VA), 1.0),
        'cache_mla_ckv': nrm(4, (DEPTH, DEC_BATCH, PAST_LEN, KV_LORA), 1.0),
        'cache_mla_krope': nrm(5, (DEPTH, DEC_BATCH, PAST_LEN, ROPE_B), 1.0),
        'state_gdn_conv': nrm(6, (DEPTH, DEC_BATCH, CONV_W - 1, QKV_C), 1.0),
        'state_gdn_s': nrm(7, (DEPTH, DEC_BATCH, H_C, DV_C, DK_C), 0.1),
        'norm_mix': gain(8, (DEPTH, D_MODEL)),
        'w_in': nrm(9, (DEPTH, D_MODEL, IN_COLS), D_MODEL ** -0.5),
        'diff_lambda': nrm(10, (DEPTH, 4, DH_A), 0.1),
        'diff_subln': gain(11, (DEPTH, VA)),
        'mla_q_norm': gain(12, (DEPTH, Q_LORA)),
        'mla_w_uq': nrm(13, (DEPTH, Q_LORA, H_B * (NOPE_B + ROPE_B)), Q_LORA ** -0.5),
        'mla_kv_norm': gain(14, (DEPTH, KV_LORA)),
        'mla_w_ukv': nrm(15, (DEPTH, KV_LORA, H_B * (NOPE_B + V_B)), KV_LORA ** -0.5),
        'gdn_conv': nrm(16, (DEPTH, CONV_W, QKV_C), CONV_W ** -0.5),
        'gdn_a_log': jnp.log(jax.random.uniform(ks[18], (DEPTH, H_C), jnp.float32, 1.0, 16.0)),
        'gdn_dt_bias': dt + jnp.log(-jnp.expm1(-dt)),
        'gdn_norm': gain(19, (DEPTH, DV_C)),
        'w_branch': nrm(20, (DEPTH, N_BRANCH, BRANCH_W, D_MODEL), BRANCH_W ** -0.5),
        'w_out': nrm(21, (DEPTH, D_MODEL, D_MODEL), D_MODEL ** -0.5),
        'norm_ffn': gain(22, (DEPTH, D_MODEL)),
        'router_group': nrm(23, (DEPTH, D_MODEL, N_GROUPS), D_MODEL ** -0.5),
        'router_group_bias': nrm(24, (DEPTH, N_GROUPS), 0.01),
        'router_expert': nrm(25, (DEPTH, D_MODEL, N_EXPERTS), D_MODEL ** -0.5),
        'router_expert_bias': nrm(26, (DEPTH, N_EXPERTS), 0.01),
        'expert_w_gate': nrm(27, (DEPTH, N_EXPERTS, D_MODEL, D_EXPERT), D_MODEL ** -0.5),
        'expert_w_up': nrm(28, (DEPTH, N_EXPERTS, D_MODEL, D_EXPERT), D_MODEL ** -0.5),
        'expert_w_down': nrm(29, (DEPTH, N_EXPERTS, D_EXPERT, D_MODEL), D_EXPERT ** -0.5),
        'norm_final': gain(30, (D_MODEL,)),
    }


def reference(x_prompt, x_sample, cache_diff_k, cache_diff_v, cache_mla_ckv, cache_mla_krope,
              state_gdn_conv, state_gdn_s, norm_mix, w_in, diff_lambda, diff_subln,
              mla_q_norm, mla_w_uq, mla_kv_norm, mla_w_ukv, gdn_conv, gdn_a_log, gdn_dt_bias,
              gdn_norm, w_branch, w_out, norm_ffn, router_group, router_group_bias,
              router_expert, router_expert_bias, expert_w_gate, expert_w_up, expert_w_down,
              norm_final):
    splits = np.cumsum(IN_SIZES)[:-1]

    def layer(x, pos, past, l):
        B, T, _ = x.shape
        h = rmsnorm(x, norm_mix[l])
        (a_q, a_k, a_v, b_dq, b_dkv, b_kr, c_qkv, c_z, c_b, c_a, g_in) = jnp.split(h @ w_in[l], splits, axis=-1)

        qa = apply_rope(a_q.reshape(B, T, H_A, 2, DH_A), pos, ROT_A)
        ka = apply_rope(a_k.reshape(B, T, H_A, 2, DH_A), pos, ROT_A)
        va = a_v.reshape(B, T, H_A, VA)
        cq = rmsnorm(b_dq, mla_q_norm[l])
        qb = (cq @ mla_w_uq[l]).reshape(B, T, H_B, NOPE_B + ROPE_B)
        qb = jnp.concatenate([qb[..., :NOPE_B], apply_rope(qb[..., NOPE_B:], pos, ROPE_B)], axis=-1)
        ckv = rmsnorm(b_dkv, mla_kv_norm[l])
        kr = apply_rope(b_kr, pos, ROPE_B)

        if past is None:
            k_pos = pos
            ka_all, va_all, ckv_all, kr_all = ka, va, ckv, kr
            conv_prev = jnp.zeros((B, CONV_W - 1, QKV_C), x.dtype)
            s0 = jnp.zeros((B, H_C, DV_C, DK_C), jnp.float32)
        else:
            pk, pv, pc, pr, pconv, ps = past
            k_pos = jnp.arange(pk.shape[1] + T)
            ka_all = jnp.concatenate([pk.astype(ka.dtype), ka], axis=1)
            va_all = jnp.concatenate([pv.astype(va.dtype), va], axis=1)
            ckv_all = jnp.concatenate([pc.astype(ckv.dtype), ckv], axis=1)
            kr_all = jnp.concatenate([pr.astype(kr.dtype), kr], axis=1)
            conv_prev = pconv.astype(x.dtype)
            s0 = ps.astype(jnp.float32)
        Tk = ka_all.shape[1]

        lq1, lk1, lq2, lk2 = diff_lambda[l].astype(jnp.float32)
        lam_init = 0.8 - 0.6 * math.exp(-0.3 * l)
        lam = jnp.exp(jnp.sum(lq1 * lk1)) - jnp.exp(jnp.sum(lq2 * lk2)) + lam_init
        oa = diff_attention(qa, ka_all, va_all, pos, k_pos, lam)
        oa = (rmsnorm(oa, diff_subln[l]) * (1.0 - lam_init)).reshape(B, T, BRANCH_W)

        kvb = (ckv_all @ mla_w_ukv[l]).reshape(B, Tk, H_B, NOPE_B + V_B)
        kb = jnp.concatenate([kvb[..., :NOPE_B],
                              jnp.broadcast_to(kr_all[:, :, None, :], (B, Tk, H_B, ROPE_B))], axis=-1)
        ob = softmax_attention(qb, kb, kvb[..., NOPE_B:], pos, k_pos).reshape(B, T, BRANCH_W)

        xpad = jnp.concatenate([conv_prev, c_qkv], axis=1)
        new_conv = xpad[:, -(CONV_W - 1):]
        conv = jax.nn.silu(lax.conv_general_dilated(
            xpad, gdn_conv[l][:, None, :].astype(xpad.dtype), (1,), 'VALID',
            dimension_numbers=('NWC', 'WIO', 'NWC'), feature_group_count=QKV_C))
        qc, kc, vc = jnp.split(conv, [H_C * DK_C, 2 * H_C * DK_C], axis=-1)
        qc = l2norm(qc.reshape(B, T, H_C, DK_C).astype(jnp.float32)) * (DK_C ** -0.5)
        kc = l2norm(kc.reshape(B, T, H_C, DK_C).astype(jnp.float32))
        vc = vc.reshape(B, T, H_C, DV_C).astype(jnp.float32)
        beta = jax.nn.sigmoid(c_b.astype(jnp.float32))
        g = -jnp.exp(gdn_a_log[l].astype(jnp.float32)) * jax.nn.softplus(
            c_a.astype(jnp.float32) + gdn_dt_bias[l].astype(jnp.float32))
        oc, s_new = gated_delta_rule(qc, kc, vc, beta, g, s0)
        oc = rmsnorm(oc.astype(x.dtype), gdn_norm[l]) * jax.nn.silu(c_z.reshape(B, T, H_C, DV_C))
        oc = oc.reshape(B, T, BRANCH_W)

        o = jnp.stack([oa, ob, oc], axis=2)
        gates = jax.nn.sigmoid(g_in.reshape(B, T, N_BRANCH, D_MODEL))
        merged = jnp.sum(jnp.einsum('btnw,nwd->btnd', o, w_branch[l]) * gates, axis=2)
        x = x + merged @ w_out[l]

        x = x + hier_moe(rmsnorm(x, norm_ffn[l]), router_group[l], router_group_bias[l],
                         router_expert[l], router_expert_bias[l],
                         expert_w_gate[l], expert_w_up[l], expert_w_down[l])
        return x, (ka, va, ckv, kr, new_conv, s_new.astype(x.dtype))

    pos_p = jnp.arange(x_prompt.shape[1])
    pos_s = cache_diff_k.shape[2] + jnp.arange(x_sample.shape[1])
    yp, ys = x_prompt, x_sample
    sp, ss = [], []
    for l in range(DEPTH):
        yp, st_p = layer(yp, pos_p, None, l)
        sp.append(st_p)
        past = (cache_diff_k[l], cache_diff_v[l], cache_mla_ckv[l], cache_mla_krope[l],
                state_gdn_conv[l], state_gdn_s[l])
        ys, st_s = layer(ys, pos_s, past, l)
        ss.append(st_s)
    y_prompt = rmsnorm(yp, norm_final)
    y_sample = rmsnorm(ys, norm_final)

    def stk(sts, i):
        return jnp.stack([s[i] for s in sts])

    return (y_prompt, y_sample,
            stk(sp, 0), stk(sp, 1), stk(sp, 2), stk(sp, 3), stk(sp, 4), stk(sp, 5),
            stk(ss, 0), stk(ss, 1), stk(ss, 2), stk(ss, 3), stk(ss, 4), stk(ss, 5))
```

```python
import functools
import math

import jax
import jax.numpy as jnp
from jax import lax
from jax.experimental import pallas as pl
from jax.experimental.pallas import tpu as pltpu

F32 = jnp.float32
BF16 = jnp.bfloat16

D_MODEL = 2048
CHUNK = 64
ROPE_THETA = 500000.0
EPS = 1e-6
N_BRANCH = 3
H_A = 8
DH_A = 64
VA = 2 * DH_A
ROT_A = DH_A // 4
H_B = 8
Q_LORA = 512
KV_LORA = 256
NOPE_B = 64
ROPE_B = 32
V_B = 128
H_C = 8
DK_C = 128
DV_C = 128
CONV_W = 4
QKV_C = 2 * H_C * DK_C + H_C * DV_C
BRANCH_W = H_A * VA
N_GROUPS = 4
EXPERTS_PER_GROUP = 4
N_EXPERTS = N_GROUPS * EXPERTS_PER_GROUP
D_EXPERT = 512

LANES = 128
VMEM_LIMIT_BYTES = 48 * 1024 * 1024
NEG_BIG = -1e30

COL_AQ, COL_AK, COL_AV = 0, 1024, 2048
COL_DQ, COL_DKV, COL_MISC = 3072, 3584, 3840
COL_CQ, COL_CK, COL_CV, COL_CZ = 4096, 5120, 6144, 7168
SLAB_F = 8192
MISC_B0 = ROPE_B
MISC_A0 = ROPE_B + H_C

MOE_ROWS = 256


def _cparams(*sem):
    return pltpu.CompilerParams(dimension_semantics=sem, vmem_limit_bytes=VMEM_LIMIT_BYTES)


def _sigmoid(x):
    return 1.0 / (1.0 + jnp.exp(-x))


def _silu(x):
    return x * _sigmoid(x)


def _rmsnorm_kernel(x_ref, w_ref, o_ref):
    x = x_ref[...]
    ms = jnp.mean(x * x, axis=-1, keepdims=True)
    o_ref[...] = (x * lax.rsqrt(ms + EPS) * w_ref[...]).astype(o_ref.dtype)


def rmsnorm_rows(x, w, out_dtype, tm=512):
    m, d = x.shape
    return pl.pallas_call(
        _rmsnorm_kernel,
        grid=(m // tm,),
        in_specs=[pl.BlockSpec((tm, d), lambda i: (i, 0)),
                  pl.BlockSpec((1, d), lambda i: (0, 0))],
        out_specs=pl.BlockSpec((tm, d), lambda i: (i, 0)),
        out_shape=jax.ShapeDtypeStruct((m, d), out_dtype),
        compiler_params=_cparams("parallel"),
        name="rmsnorm",
    )(x, w.reshape(1, d))


def _mm_kernel(a_ref, b_ref, o_ref, *, act):
    acc = jnp.dot(a_ref[...], b_ref[...], preferred_element_type=F32)
    if act == "sigmoid":
        acc = _sigmoid(acc)
    o_ref[...] = acc.astype(o_ref.dtype)


def _mm_res_kernel(a_ref, b_ref, r_ref, o_ref):
    acc = jnp.dot(a_ref[...], b_ref[...], preferred_element_type=F32)
    o_ref[...] = r_ref[...] + acc


def matmul(a, b, out_dtype, act=None, residual=None, tm=512, tn=1024, name="mm"):
    m, k = a.shape
    n = b.shape[1]
    in_specs = [pl.BlockSpec((tm, k), lambda i, j: (i, 0)),
                pl.BlockSpec((k, tn), lambda i, j: (0, j))]
    args = [a, b]
    if residual is None:
        body = functools.partial(_mm_kernel, act=act)
    else:
        body = _mm_res_kernel
        in_specs.append(pl.BlockSpec((tm, tn), lambda i, j: (i, j)))
        args.append(residual)
    return pl.pallas_call(
        body,
        grid=(m // tm, n // tn),
        in_specs=in_specs,
        out_specs=pl.BlockSpec((tm, tn), lambda i, j: (i, j)),
        out_shape=jax.ShapeDtypeStruct((m, n), out_dtype),
        compiler_params=_cparams("parallel", "parallel"),
        name=name,
    )(*args)


def _rope_tables(pos, period, offset, half):
    rot = 2 * half
    inv = ROPE_THETA ** (-jnp.arange(half, dtype=F32) * 2.0 / rot)
    ang = pos.astype(F32)[:, None] * inv[None, :]
    cos, sin = jnp.cos(ang), jnp.sin(ang)
    lane = jnp.arange(LANES)
    r = (lane % period) - offset
    first = (r >= 0) & (r < half)
    second = (r >= half) & (r < rot)
    idx = jnp.clip(jnp.where(second, r - half, r), 0, half - 1)
    cos_l, sin_l = cos[:, idx], sin[:, idx]
    c = jnp.where((first | second)[None, :], cos_l, 1.0)
    sa = jnp.where(first[None, :], -sin_l, 0.0)
    sb = jnp.where(second[None, :], sin_l, 0.0)
    return c, sa, sb


def _rope_lanes(x, c, sa, sb, half):
    return (x * c + pltpu.roll(x, LANES - half, axis=1) * sa
            + pltpu.roll(x, half, axis=1) * sb)


def _prep_kernel(aq_ref, ak_ref, av_ref, dq_ref, dkv_ref, misc_ref,
                 ca_ref, saa_ref, sba_ref, cb_ref, sab_ref, sbb_ref, ck_ref, sak_ref, sbk_ref,
                 qn_ref, wuq_ref, kvn_ref,
                 qa_o, ka_o, kabf_o, va_o, vabf_o, qb_o, ckv_o, kr_o):
    ca, saa, sba = ca_ref[...], saa_ref[...], sba_ref[...]
    for c in range(H_A):
        sl = slice(c * LANES, (c + 1) * LANES)
        q = _rope_lanes(aq_ref[:, sl], ca, saa, sba, ROT_A // 2)
        qa_o[:, sl] = (q * (DH_A ** -0.5)).astype(BF16)
        k = _rope_lanes(ak_ref[:, sl], ca, saa, sba, ROT_A // 2)
        ka_o[:, sl] = k
        kabf_o[:, sl] = k.astype(BF16)
    v = av_ref[...]
    va_o[...] = v
    vabf_o[...] = v.astype(BF16)

    x = dq_ref[...]
    cq = x * lax.rsqrt(jnp.mean(x * x, axis=-1, keepdims=True) + EPS) * qn_ref[...]
    qb = jnp.dot(cq.astype(BF16), wuq_ref[...], preferred_element_type=F32)
    cb, sab, sbb = cb_ref[...], sab_ref[...], sbb_ref[...]
    scale_b = (NOPE_B + ROPE_B) ** -0.5
    for c in range(H_B):
        sl = slice(c * LANES, (c + 1) * LANES)
        q = _rope_lanes(qb[:, sl], cb, sab, sbb, ROPE_B // 2)
        qb_o[:, sl] = (q * scale_b).astype(BF16)

    x = dkv_ref[...]
    ckv_o[...] = x * lax.rsqrt(jnp.mean(x * x, axis=-1, keepdims=True) + EPS) * kvn_ref[...]
    kr = _rope_lanes(misc_ref[...], ck_ref[...], sak_ref[...], sbk_ref[...], ROPE_B // 2)
    kr_o[...] = kr[:, :ROPE_B]


def mixer_prep(slab, tabs, q_norm, w_uq_pad, kv_norm, tm=256):
    m = slab.shape[0]

    def col(width, start):
        blk = start // width
        return pl.BlockSpec((tm, width), lambda i: (i, blk))

    def row(width):
        return pl.BlockSpec((tm, width), lambda i: (i, 0))

    def full(shape):
        return pl.BlockSpec(shape, lambda i: (0,) * len(shape))

    in_specs = [col(1024, COL_AQ), col(1024, COL_AK), col(1024, COL_AV),
                col(Q_LORA, COL_DQ), col(KV_LORA, COL_DKV), col(LANES, COL_MISC)]
    in_specs += [row(LANES)] * 9
    in_specs += [full((1, Q_LORA)), full((Q_LORA, 1024)), full((1, KV_LORA))]
    out_shape = [jax.ShapeDtypeStruct((m, 1024), BF16),
                 jax.ShapeDtypeStruct((m, 1024), F32),
                 jax.ShapeDtypeStruct((m, 1024), BF16),
                 jax.ShapeDtypeStruct((m, 1024), F32),
                 jax.ShapeDtypeStruct((m, 1024), BF16),
                 jax.ShapeDtypeStruct((m, 1024), BF16),
                 jax.ShapeDtypeStruct((m, KV_LORA), F32),
                 jax.ShapeDtypeStruct((m, ROPE_B), F32)]
    out_specs = [row(1024)] * 6 + [row(KV_LORA), row(ROPE_B)]
    return pl.pallas_call(
        _prep_kernel,
        grid=(m // tm,),
        in_specs=in_specs,
        out_specs=out_specs,
        out_shape=out_shape,
        compiler_params=_cparams("parallel"),
        name="mixer_prep",
    )(slab, slab, slab, slab, slab, slab, *tabs,
      q_norm.reshape(1, Q_LORA), w_uq_pad, kv_norm.reshape(1, KV_LORA))


def _expand_kernel(ckv_ref, kr_ref, wuk_ref, pk_ref, wuv_ref, kb_o, vb_o):
    ckv = ckv_ref[...].astype(BF16)
    kr = kr_ref[...].astype(BF16)
    kb = (jnp.dot(ckv, wuk_ref[...], preferred_element_type=F32)
          + jnp.dot(kr, pk_ref[...], preferred_element_type=F32))
    kb_o[...] = kb.astype(BF16)
    vb_o[...] = jnp.dot(ckv, wuv_ref[...], preferred_element_type=F32).astype(BF16)


def mla_expand(ckv, kr, w_uk_pad, p_kr, w_uv, tm=512):
    m = ckv.shape[0]
    return pl.pallas_call(
        _expand_kernel,
        grid=(m // tm,),
        in_specs=[pl.BlockSpec((tm, KV_LORA), lambda i: (i, 0)),
                  pl.BlockSpec((tm, ROPE_B), lambda i: (i, 0)),
                  pl.BlockSpec((KV_LORA, 1024), lambda i: (0, 0)),
                  pl.BlockSpec((ROPE_B, 1024), lambda i: (0, 0)),
                  pl.BlockSpec((KV_LORA, 1024), lambda i: (0, 0))],
        out_specs=[pl.BlockSpec((tm, 1024), lambda i: (i, 0))] * 2,
        out_shape=[jax.ShapeDtypeStruct((m, 1024), BF16)] * 2,
        compiler_params=_cparams("parallel"),
        name="mla_expand",
    )(ckv, kr, w_uk_pad, p_kr, w_uv)


def _attn_kernel(*refs, diff, tq, tk, q_off, nk, lam_init):
    if diff:
        q_ref, k_ref, v_ref, lamp_ref, subln_ref, o_ref, m_sc, l_sc, acc_sc = refs
    else:
        q_ref, k_ref, v_ref, o_ref, m_sc, l_sc, acc_sc = refs
    n_maps = 2 if diff else 1
    i = pl.program_id(2)
    j = pl.program_id(3)

    @pl.when(j == 0)
    def _():
        m_sc[...] = jnp.full(m_sc.shape, NEG_BIG, F32)
        l_sc[...] = jnp.zeros(l_sc.shape, F32)
        acc_sc[...] = jnp.zeros(acc_sc.shape, F32)

    q_chunk_max = (q_off + i * tq + tq - 1) // CHUNK

    @pl.when((j * tk) // CHUNK <= q_chunk_max)
    def _():
        q = q_ref[0]
        k = k_ref[0]
        v = v_ref[0]
        q_chunk = (q_off + i * tq + lax.broadcasted_iota(jnp.int32, (tq, 1), 0)) // CHUNK
        k_chunk = (j * tk + lax.broadcasted_iota(jnp.int32, (1, tk), 1)) // CHUNK
        visible = q_chunk >= k_chunk
        lane = lax.broadcasted_iota(jnp.int32, (1, LANES), 1)
        for mp in range(n_maps):
            if diff:
                in_map = (lane < DH_A) if mp == 0 else (lane >= DH_A)
                qm = jnp.where(in_map, q, jnp.zeros_like(q))
            else:
                qm = q
            s = lax.dot_general(qm, k, (((1,), (1,)), ((), ())), preferred_element_type=F32)
            s = jnp.where(visible, s, NEG_BIG)
            m_prev = m_sc[mp]
            m_new = jnp.maximum(m_prev, jnp.max(s, axis=-1, keepdims=True))
            alpha = jnp.exp(m_prev - m_new)
            p = jnp.exp(s - m_new)
            l_sc[mp] = alpha * l_sc[mp] + jnp.sum(p, axis=-1, keepdims=True)
            acc_sc[mp] = alpha * acc_sc[mp] + jnp.dot(p.astype(BF16), v, preferred_element_type=F32)
            m_sc[mp] = m_new

    @pl.when(j == nk - 1)
    def _():
        if diff:
            lp = lamp_ref[...]
            lam = (jnp.exp(jnp.sum(lp[0:1] * lp[1:2], axis=-1, keepdims=True))
                   - jnp.exp(jnp.sum(lp[2:3] * lp[3:4], axis=-1, keepdims=True)) + lam_init)
            o = acc_sc[0] / l_sc[0] - lam * (acc_sc[1] / l_sc[1])
            o = o * lax.rsqrt(jnp.mean(o * o, axis=-1, keepdims=True) + EPS) * subln_ref[...]
            o = o * (1.0 - lam_init)
        else:
            o = acc_sc[0] / l_sc[0]
        o_ref[0] = o.astype(o_ref.dtype)


def attention(q, k, v, *, q_off, tq, tk, diff, lamp=None, subln=None, lam_init=0.0, name="attn"):
    b, t_q, _ = q.shape
    t_k = k.shape[1]
    nq, nk = t_q // tq, t_k // tk
    n_heads = H_A

    def kv_map(bi, h, i, j):
        last = ((q_off + i * tq + tq - 1) // CHUNK * CHUNK) // tk
        return (bi, jnp.minimum(j, last), h)

    in_specs = [pl.BlockSpec((1, tq, LANES), lambda bi, h, i, j: (bi, i, h)),
                pl.BlockSpec((1, tk, LANES), kv_map),
                pl.BlockSpec((1, tk, LANES), kv_map)]
    args = [q, k, v]
    n_maps = 1
    if diff:
        n_maps = 2
        in_specs += [pl.BlockSpec((4, DH_A), lambda bi, h, i, j: (0, 0)),
                     pl.BlockSpec((1, VA), lambda bi, h, i, j: (0, 0))]
        args += [lamp, subln.reshape(1, VA)]
    return pl.pallas_call(
        functools.partial(_attn_kernel, diff=diff, tq=tq, tk=tk, q_off=q_off, nk=nk, lam_init=lam_init),
        grid=(b, n_heads, nq, nk),
        in_specs=in_specs,
        out_specs=pl.BlockSpec((1, tq, LANES), lambda bi, h, i, j: (bi, i, h)),
        out_shape=jax.ShapeDtypeStruct((b, t_q, n_heads * LANES), BF16),
        scratch_shapes=[pltpu.VMEM((n_maps, tq, 1), F32),
                        pltpu.VMEM((n_maps, tq, 1), F32),
                        pltpu.VMEM((n_maps, tq, LANES), F32)],
        compiler_params=_cparams("parallel", "parallel", "parallel", "arbitrary"),
        name=name,
    )(*args)


def _mm_nt(a, b):
    return lax.dot_general(a, b, (((1,), (1,)), ((), ())), preferred_element_type=F32)


def _mm_nn(a, b):
    return jnp.dot(a, b, preferred_element_type=F32)


def _mm_tn(a, b):
    return lax.dot_general(a, b, (((0,), (0,)), ((), ())), preferred_element_type=F32)


def _hp(a, b):
    return jnp.dot(a, b, preferred_element_type=F32, precision=lax.Precision.HIGHEST)


def _shift_rows(x, tail, s):
    xs = pltpu.roll(x, s, axis=0)
    ts = pltpu.roll(tail, s, axis=0)
    row = lax.broadcasted_iota(jnp.int32, (8, 1), 0)
    top = jnp.where(row < s, ts, xs[0:8])
    return jnp.concatenate([top, xs[8:]], axis=0)


def _gdn_kernel(cq_ref, ck_ref, cv_ref, cz_ref, misc_ref, convp_ref, s0_ref,
                kern_ref, alog_ref, dtb_ref, norm_ref,
                o_ref, s_o, s_sc, tail_sc, *, n_chunks):
    n = pl.program_id(1)
    L = CHUNK

    @pl.when(n == 0)
    def _():
        s_sc[...] = s0_ref[0]
        tail_sc[...] = jnp.zeros(tail_sc.shape, F32)
        tail_sc[8 - (CONV_W - 1):8, :] = convp_ref[0]

    conv = []
    for part, ref in enumerate((cq_ref, ck_ref, cv_ref)):
        cs = slice(part * 1024, (part + 1) * 1024)
        x = ref[...]
        tail = tail_sc[:, cs]
        acc = x * kern_ref[CONV_W - 1:CONV_W, cs]
        for s in range(1, CONV_W):
            acc = acc + _shift_rows(x, tail, s) * kern_ref[CONV_W - 1 - s:CONV_W - s, cs]
        conv.append(_silu(acc))
        tail_sc[:, cs] = x[L - 8:L]
    qc, kc, vc = conv

    misc = misc_ref[...]
    beta_all = _sigmoid(misc)
    za = misc + dtb_ref[...]
    softplus = jnp.maximum(za, 0.0) + jnp.log(1.0 + jnp.exp(-jnp.abs(za)))
    g_all = -jnp.exp(alog_ref[...]) * softplus
    row = lax.broadcasted_iota(jnp.int32, (L, 1), 0)
    gcum_all = g_all
    for s in (1, 2, 4, 8, 16, 32):
        gcum_all = gcum_all + jnp.where(row >= s, pltpu.roll(gcum_all, s, axis=0), 0.0)
    lane = lax.broadcasted_iota(jnp.int32, (1, LANES), 1)
    is_beta = (lane >= MISC_B0) & (lane < MISC_B0 + H_C)
    packed = jnp.where(is_beta, beta_all, gcum_all)
    packed_t = jnp.concatenate([packed, jnp.zeros((LANES - L, LANES), F32)], axis=0).T

    ri = lax.broadcasted_iota(jnp.int32, (L, L), 0)
    ci = lax.broadcasted_iota(jnp.int32, (L, L), 1)
    incl = ri >= ci
    strict = ri > ci
    eye = (ri == ci).astype(F32)
    diag16 = (ri // 16) == (ci // 16)

    for h in range(H_C):
        hs = slice(h * LANES, (h + 1) * LANES)
        qh = qc[:, hs]
        kh = kc[:, hs]
        vh = vc[:, hs]
        qh = qh * lax.rsqrt(jnp.sum(qh * qh, axis=-1, keepdims=True) + EPS) * (DK_C ** -0.5)
        kh = kh * lax.rsqrt(jnp.sum(kh * kh, axis=-1, keepdims=True) + EPS)
        beta = beta_all[:, MISC_B0 + h:MISC_B0 + h + 1]
        gcum = gcum_all[:, MISC_A0 + h:MISC_A0 + h + 1]
        gcum_r = packed_t[MISC_A0 + h:MISC_A0 + h + 1, 0:L]
        seg = gcum - gcum_r
        decay = jnp.where(incl, jnp.exp(jnp.where(incl, seg, 0.0)), 0.0)
        gam = jnp.exp(gcum)
        qb16, kb16 = qh.astype(BF16), kh.astype(BF16)
        kk = _mm_nt(kb16, kb16)
        a_mat = jnp.where(strict, beta * kk * decay, 0.0)
        pq = _mm_nt(qb16, kb16) * decay

        a_d = jnp.where(diag16, a_mat, 0.0)
        a_lo = a_mat - a_d
        a2 = _hp(a_d, a_d)
        a4 = _hp(a2, a2)
        a8 = _hp(a4, a4)
        d_inv = _hp(_hp(eye - a_d, eye + a2), _hp(eye + a4, eye + a8))
        nmat = _hp(d_inv, a_lo)
        n2 = _hp(nmat, nmat)
        t_inv = _hp(_hp(eye - nmat, eye + n2), d_inv)
        rhs = jnp.concatenate([vh * beta, kh * (beta * gam)], axis=-1)
        sol = _hp(t_inv, rhs)
        u0, w = sol[:, :DV_C], sol[:, DV_C:]

        s_prev = s_sc[h]
        s16 = s_prev.astype(BF16)
        u = u0 - _mm_nt(w.astype(BF16), s16)
        o = gam * _mm_nt(qb16, s16) + _mm_nn(pq.astype(BF16), u.astype(BF16))
        gcum_end = gcum[L - 1:L, :]
        tend = jnp.exp(gcum_end - gcum)
        s_new = jnp.exp(gcum_end) * s_prev + _mm_tn((u * tend).astype(BF16), kb16)
        s_sc[h] = s_new

        o = o * lax.rsqrt(jnp.mean(o * o, axis=-1, keepdims=True) + EPS) * norm_ref[...]
        o_ref[:, hs] = (o * _silu(cz_ref[:, hs])).astype(o_ref.dtype)

    @pl.when(n == n_chunks - 1)
    def _():
        s_o[0] = s_sc[...]


def gated_delta(slab, conv_prev, s0, kern, alog_row, dtb_row, norm_w, *, row_blk0, n_streams, n_chunks):
    m = slab.shape[0]

    def col(start):
        blk = start // 1024
        return pl.BlockSpec((CHUNK, 1024), lambda b, n: (row_blk0 + b * n_chunks + n, blk))

    def full(shape):
        return pl.BlockSpec(shape, lambda b, n: (0,) * len(shape))

    in_specs = [col(COL_CQ), col(COL_CK), col(COL_CV), col(COL_CZ),
                pl.BlockSpec((CHUNK, LANES), lambda b, n: (row_blk0 + b * n_chunks + n, COL_MISC // LANES)),
                pl.BlockSpec((1, CONV_W - 1, QKV_C), lambda b, n: (b, 0, 0)),
                pl.BlockSpec((1, H_C, DV_C, DK_C), lambda b, n: (b, 0, 0, 0)),
                full((CONV_W, QKV_C)), full((1, LANES)), full((1, LANES)), full((1, DV_C))]
    o, s_new = pl.pallas_call(
        functools.partial(_gdn_kernel, n_chunks=n_chunks),
        grid=(n_streams, n_chunks),
        in_specs=in_specs,
        out_specs=[pl.BlockSpec((CHUNK, 1024), lambda b, n: (b * n_chunks + n, 0)),
                   pl.BlockSpec((1, H_C, DV_C, DK_C), lambda b, n: (b, 0, 0, 0))],
        out_shape=[jax.ShapeDtypeStruct((n_streams * n_chunks * CHUNK, 1024), BF16),
                   jax.ShapeDtypeStruct((n_streams, H_C, DV_C, DK_C), F32)],
        scratch_shapes=[pltpu.VMEM((H_C, DV_C, DK_C), F32),
                        pltpu.VMEM((8, QKV_C), F32)],
        compiler_params=_cparams("parallel", "arbitrary"),
        name="gated_delta",
    )(slab, slab, slab, slab, slab, conv_prev, s0, kern, alog_row, dtb_row, norm_w.reshape(1, DV_C))
    del m
    return o, s_new


def _merge_kernel(oa_ref, ob_ref, oc_ref, wb_ref, ga_ref, gb_ref, gc_ref, o_ref):
    acc = jnp.dot(oa_ref[...], wb_ref[0], preferred_element_type=F32) * ga_ref[...].astype(F32)
    acc = acc + jnp.dot(ob_ref[...], wb_ref[1], preferred_element_type=F32) * gb_ref[...].astype(F32)
    acc = acc + jnp.dot(oc_ref[...], wb_ref[2], preferred_element_type=F32) * gc_ref[...].astype(F32)
    o_ref[...] = acc.astype(o_ref.dtype)


def branch_merge(oa, ob, oc, w_branch, gates, tm=512, tn=1024):
    m = oa.shape[0]
    nj = D_MODEL // tn

    def gate_spec(nb):
        return pl.BlockSpec((tm, tn), lambda i, j: (i, nb * nj + j))

    return pl.pallas_call(
        _merge_kernel,
        grid=(m // tm, nj),
        in_specs=[pl.BlockSpec((tm, BRANCH_W), lambda i, j: (i, 0))] * 3
                 + [pl.BlockSpec((N_BRANCH, BRANCH_W, tn), lambda i, j: (0, 0, j))]
                 + [gate_spec(0), gate_spec(1), gate_spec(2)],
        out_specs=pl.BlockSpec((tm, tn), lambda i, j: (i, j)),
        out_shape=jax.ShapeDtypeStruct((m, D_MODEL), BF16),
        compiler_params=_cparams("parallel", "parallel"),
        name="branch_merge",
    )(oa, ob, oc, w_branch, gates, gates, gates)


def _router_kernel(x_ref, nw_ref, wr_ref, br_ref, h_o, info_o):
    x = x_ref[...]
    h = (x * lax.rsqrt(jnp.mean(x * x, axis=-1, keepdims=True) + EPS) * nw_ref[...])
    hb = h.astype(BF16)
    h_o[...] = h
    logits = jnp.dot(hb, wr_ref[...], preferred_element_type=F32) + br_ref[...]
    lane = lax.broadcasted_iota(jnp.int32, logits.shape, 1)
    is_g = lane < N_GROUPS
    lg = jnp.where(is_g, logits, NEG_BIG)
    eg = jnp.where(is_g, jnp.exp(lg - jnp.max(lg, axis=-1, keepdims=True)), 0.0)
    gp = eg / jnp.sum(eg, axis=-1, keepdims=True)
    pg = jnp.max(gp, axis=-1, keepdims=True)
    grp = jnp.min(jnp.where(is_g & (gp == pg), lane, LANES), axis=-1, keepdims=True)
    lo = N_GROUPS + grp * EXPERTS_PER_GROUP
    is_e = (lane >= lo) & (lane < lo + EXPERTS_PER_GROUP)
    le = jnp.where(is_e, logits, NEG_BIG)
    ee = jnp.where(is_e, jnp.exp(le - jnp.max(le, axis=-1, keepdims=True)), 0.0)
    ep = ee / jnp.sum(ee, axis=-1, keepdims=True)
    v1 = jnp.max(jnp.where(is_e, ep, -1.0), axis=-1, keepdims=True)
    i1 = jnp.min(jnp.where(is_e & (ep == v1), lane, LANES), axis=-1, keepdims=True)
    rest = is_e & (lane != i1)
    v2 = jnp.max(jnp.where(rest, ep, -1.0), axis=-1, keepdims=True)
    i2 = jnp.min(jnp.where(rest & (ep == v2), lane, LANES), axis=-1, keepdims=True)
    den = v1 + v2
    w1 = pg * v1 / den
    w2 = pg * v2 / den
    info = jnp.where(lane == 0, (i1 - N_GROUPS).astype(F32), 0.0)
    info = jnp.where(lane == 1, (i2 - N_GROUPS).astype(F32), info)
    info = jnp.where(lane == 2, w1, info)
    info = jnp.where(lane == 3, w2, info)
    info_o[...] = info


def moe_router(x, norm_w, w_router, b_router, tm=512):
    m = x.shape[0]
    return pl.pallas_call(
        _router_kernel,
        grid=(m // tm,),
        in_specs=[pl.BlockSpec((tm, D_MODEL), lambda i: (i, 0)),
                  pl.BlockSpec((1, D_MODEL), lambda i: (0, 0)),
                  pl.BlockSpec((D_MODEL, LANES), lambda i: (0, 0)),
                  pl.BlockSpec((1, LANES), lambda i: (0, 0))],
        out_specs=[pl.BlockSpec((tm, D_MODEL), lambda i: (i, 0)),
                   pl.BlockSpec((tm, LANES), lambda i: (i, 0))],
        out_shape=[jax.ShapeDtypeStruct((m, D_MODEL), F32),
                   jax.ShapeDtypeStruct((m, LANES), F32)],
        compiler_params=_cparams("parallel"),
        name="moe_router",
    )(x, norm_w.reshape(1, D_MODEL), w_router, b_router)


def _gather_kernel(*refs, n_idx, rows, has_res):
    idx_refs = refs[:n_idx]
    src_ref = refs[n_idx]
    pos = n_idx + 1
    res_ref = None
    if has_res:
        res_ref = refs[pos]
        pos += 1
    o_ref = refs[pos]
    bufs = refs[pos + 1:pos + 1 + n_idx]
    sem = refs[pos + 1 + n_idx]

    def copy(k, r):
        return pltpu.make_async_copy(src_ref.at[pl.ds(idx_refs[k][0, 0, r], 1), :],
                                     bufs[k].at[pl.ds(r, 1), :], sem.at[k])

    def start(r, c):
        for k in range(n_idx):
            copy(k, r).start()
        return c

    lax.fori_loop(0, rows, start, 0)

    def wait(r, c):
        for k in range(n_idx):
            copy(k, r).wait()
        return c

    lax.fori_loop(0, rows, wait, 0)
    acc = bufs[0][...]
    for k in range(1, n_idx):
        acc = acc + bufs[k][...]
    if has_res:
        acc = res_ref[...] + acc
    o_ref[...] = acc


def gather_rows(src, idxs, residual=None, rows=MOE_ROWS):
    n_out = idxs[0].shape[0]
    d = src.shape[1]
    nb = n_out // rows
    n_idx = len(idxs)
    idx3 = [ix.reshape(nb, 1, rows) for ix in idxs]
    in_specs = [pl.BlockSpec((1, 1, rows), lambda i: (i, 0, 0), memory_space=pltpu.SMEM)] * n_idx
    in_specs.append(pl.BlockSpec(memory_space=pl.ANY))
    args = idx3 + [src]
    if residual is not None:
        in_specs.append(pl.BlockSpec((rows, d), lambda i: (i, 0)))
        args.append(residual)
    return pl.pallas_call(
        functools.partial(_gather_kernel, n_idx=n_idx, rows=rows, has_res=residual is not None),
        grid=(nb,),
        in_specs=in_specs,
        out_specs=pl.BlockSpec((rows, d), lambda i: (i, 0)),
        out_shape=jax.ShapeDtypeStruct((n_out, d), src.dtype),
        scratch_shapes=[pltpu.VMEM((rows, d), src.dtype)] * n_idx + [pltpu.SemaphoreType.DMA((n_idx,))],
        compiler_params=_cparams("arbitrary"),
        name="gather_rows",
    )(*args)


def _expert_kernel(be_ref, nu_ref, x_ref, rw_ref, wg_ref, wu_ref, wd_ref, o_ref):
    b = pl.program_id(0)

    @pl.when(b < nu_ref[0])
    def _():
        x = x_ref[...].astype(BF16)
        g = jnp.dot(x, wg_ref[0], preferred_element_type=F32)
        u = jnp.dot(x, wu_ref[0], preferred_element_type=F32)
        hid = _silu(g) * u * rw_ref[...]
        o_ref[...] = jnp.dot(hid.astype(BF16), wd_ref[0], preferred_element_type=F32)

    @pl.when(b >= nu_ref[0])
    def _():
        o_ref[...] = jnp.zeros(o_ref.shape, F32)


def routed_experts(xs, row_w, block_expert, n_used, w_gate, w_up, w_down, rows=MOE_ROWS):
    n_rows = xs.shape[0]
    nb = n_rows // rows
    grid_spec = pltpu.PrefetchScalarGridSpec(
        num_scalar_prefetch=2,
        grid=(nb,),
        in_specs=[pl.BlockSpec((rows, D_MODEL), lambda b, be, nu: (b, 0)),
                  pl.BlockSpec((rows, 1), lambda b, be, nu: (b, 0)),
                  pl.BlockSpec((1, D_MODEL, D_EXPERT), lambda b, be, nu: (be[b], 0, 0)),
                  pl.BlockSpec((1, D_MODEL, D_EXPERT), lambda b, be, nu: (be[b], 0, 0)),
                  pl.BlockSpec((1, D_EXPERT, D_MODEL), lambda b, be, nu: (be[b], 0, 0))],
        out_specs=pl.BlockSpec((rows, D_MODEL), lambda b, be, nu: (b, 0)),
    )
    return pl.pallas_call(
        _expert_kernel,
        grid_spec=grid_spec,
        out_shape=jax.ShapeDtypeStruct((n_rows, D_MODEL), F32),
        compiler_params=_cparams("arbitrary"),
        name="routed_experts",
    )(block_expert, n_used, xs, row_w, w_gate, w_up, w_down)


def _routing_plan(e1, e2, w1, w2, rows):
    n = e1.shape[0]
    n_slots = 2 * n + N_EXPERTS * rows
    oh1 = jax.nn.one_hot(e1, N_EXPERTS, dtype=jnp.int32)
    oh2 = jax.nn.one_hot(e2, N_EXPERTS, dtype=jnp.int32)
    both = oh1 + oh2
    before = jnp.cumsum(both, axis=0) - both
    rank1 = jnp.sum(before * oh1, axis=1)
    rank2 = jnp.sum(before * oh2, axis=1)
    count = jnp.sum(both, axis=0)
    padded = (count + rows - 1) // rows * rows
    ends = jnp.cumsum(padded)
    starts = ends - padded
    dest1 = starts[e1] + rank1
    dest2 = starts[e2] + rank2
    tok = jnp.arange(n, dtype=jnp.int32)
    row_token = jnp.zeros((n_slots,), jnp.int32).at[dest1].set(tok).at[dest2].set(tok)
    row_w = jnp.zeros((n_slots,), F32).at[dest1].set(w1).at[dest2].set(w2)
    blk_start = jnp.arange(n_slots // rows, dtype=jnp.int32) * rows
    block_expert = jnp.minimum(jnp.searchsorted(ends, blk_start, side="right"), N_EXPERTS - 1).astype(jnp.int32)
    n_used = (ends[-1] // rows).astype(jnp.int32).reshape(1)
    return row_token, row_w.reshape(n_slots, 1), block_expert, n_used, dest1.astype(jnp.int32), dest2.astype(jnp.int32)


def hier_moe_residual(x, norm_w, w_router, b_router, w_gate, w_up, w_down):
    h, info = moe_router(x, norm_w, w_router, b_router)
    e1 = info[:, 0].astype(jnp.int32)
    e2 = info[:, 1].astype(jnp.int32)
    row_token, row_w, block_expert, n_used, dest1, dest2 = _routing_plan(e1, e2, info[:, 2], info[:, 3], MOE_ROWS)
    xs = gather_rows(h, [row_token])
    ys = routed_experts(xs, row_w, block_expert, n_used, w_gate, w_up, w_down)
    return gather_rows(ys, [dest1, dest2], residual=x)


def _pack_w_in(w):
    sizes = (1024, 1024, 1024, Q_LORA, KV_LORA, ROPE_B, QKV_C, 1024, H_C, H_C, N_BRANCH * D_MODEL)
    offs = [0]
    for s in sizes:
        offs.append(offs[-1] + s)
    a_q, a_k, a_v, b_dq, b_dkv, b_kr, c_qkv, c_z, c_b, c_a, g_in = (
        w[:, offs[i]:offs[i + 1]] for i in range(len(sizes)))
    zeros = jnp.zeros((D_MODEL, 2 * LANES - ROPE_B - 2 * H_C), w.dtype)
    slab = jnp.concatenate([a_q, a_k, a_v, b_dq, b_dkv, b_kr, c_b, c_a, zeros, c_qkv, c_z], axis=1)
    return slab.astype(BF16), g_in.astype(BF16)


def _pack_mla(w_uq, w_ukv):
    wq = w_uq.reshape(Q_LORA, H_B, NOPE_B + ROPE_B)
    wq = jnp.pad(wq, ((0, 0), (0, 0), (0, LANES - NOPE_B - ROPE_B))).reshape(Q_LORA, H_B * LANES)
    wkv = w_ukv.reshape(KV_LORA, H_B, NOPE_B + V_B)
    wk = jnp.pad(wkv[:, :, :NOPE_B], ((0, 0), (0, 0), (0, LANES - NOPE_B))).reshape(KV_LORA, H_B * LANES)
    wv = wkv[:, :, NOPE_B:].reshape(KV_LORA, H_B * V_B)
    place = jnp.zeros((ROPE_B, H_B, LANES), F32)
    place = place.at[jnp.arange(ROPE_B), :, NOPE_B + jnp.arange(ROPE_B)].set(1.0)
    return wq.astype(BF16), wk.astype(BF16), place.reshape(ROPE_B, H_B * LANES).astype(BF16), wv.astype(BF16)


def _misc_row(vals, lane0):
    return jnp.zeros((1, LANES), F32).at[0, lane0:lane0 + H_C].set(vals.astype(F32))


def kernel(x_prompt, x_sample, cache_diff_k, cache_diff_v, cache_mla_ckv, cache_mla_krope, state_gdn_conv, state_gdn_s, norm_mix, w_in, diff_lambda, diff_subln, mla_q_norm, mla_w_uq, mla_kv_norm, mla_w_ukv, gdn_conv, gdn_a_log, gdn_dt_bias, gdn_norm, w_branch, w_out, norm_ffn, router_group, router_group_bias, router_expert, router_expert_bias, expert_w_gate, expert_w_up, expert_w_down, norm_final):
    bp, tp, _ = x_prompt.shape
    bs, ts, _ = x_sample.shape
    depth = w_in.shape[0]
    past = cache_diff_k.shape[2]
    n_p, n_s = bp * tp, bs * ts
    tk_s = past + ts

    x = jnp.concatenate([x_prompt.reshape(n_p, D_MODEL), x_sample.reshape(n_s, D_MODEL)], axis=0)
    pos = jnp.concatenate([jnp.tile(jnp.arange(tp), bp), jnp.tile(past + jnp.arange(ts), bs)])
    tabs = (_rope_tables(pos, DH_A, 0, ROT_A // 2)
            + _rope_tables(pos, LANES, NOPE_B, ROPE_B // 2)
            + _rope_tables(pos, LANES, 0, ROPE_B // 2))

    zero_conv = jnp.zeros((bp, CONV_W - 1, QKV_C), F32)
    zero_state = jnp.zeros((bp, H_C, DV_C, DK_C), F32)
    st_p, st_s = [], []
    for l in range(depth):
        lam_init = 0.8 - 0.6 * math.exp(-0.3 * l)
        w_slab, w_gates = _pack_w_in(w_in[l])
        wq, wk, p_kr, wv = _pack_mla(mla_w_uq[l], mla_w_ukv[l])

        h = rmsnorm_rows(x, norm_mix[l], BF16)
        slab = matmul(h, w_slab, F32, name="proj_in")
        gates = matmul(h, w_gates, BF16, act="sigmoid", name="proj_gates")
        qa, ka, ka_bf, va, va_bf, qb, ckv, kr = mixer_prep(slab, tabs, mla_q_norm[l], wq, mla_kv_norm[l])

        oa_p = attention(qa[:n_p].reshape(bp, tp, 1024), ka_bf[:n_p].reshape(bp, tp, 1024),
                         va_bf[:n_p].reshape(bp, tp, 1024), q_off=0, tq=256, tk=256, diff=True,
                         lamp=diff_lambda[l], subln=diff_subln[l], lam_init=lam_init, name="diff_attn_p")
        ka_all = jnp.concatenate([cache_diff_k[l].reshape(bs, past, 1024).astype(BF16),
                                  ka_bf[n_p:].reshape(bs, ts, 1024)], axis=1)
        va_all = jnp.concatenate([cache_diff_v[l].reshape(bs, past, 1024).astype(BF16),
                                  va_bf[n_p:].reshape(bs, ts, 1024)], axis=1)
        oa_s = attention(qa[n_p:].reshape(bs, ts, 1024), ka_all, va_all, q_off=past, tq=ts, tk=tk_s,
                         diff=True, lamp=diff_lambda[l], subln=diff_subln[l], lam_init=lam_init,
                         name="diff_attn_s")
        oa = jnp.concatenate([oa_p.reshape(n_p, 1024), oa_s.reshape(n_s, 1024)], axis=0)

        kb_p, vb_p = mla_expand(ckv[:n_p], kr[:n_p], wk, p_kr, wv)
        ob_p = attention(qb[:n_p].reshape(bp, tp, 1024), kb_p.reshape(bp, tp, 1024),
                         vb_p.reshape(bp, tp, 1024), q_off=0, tq=256, tk=256, diff=False, name="mla_attn_p")
        ckv_all = jnp.concatenate([cache_mla_ckv[l], ckv[n_p:].reshape(bs, ts, KV_LORA)], axis=1)
        kr_all = jnp.concatenate([cache_mla_krope[l], kr[n_p:].reshape(bs, ts, ROPE_B)], axis=1)
        kb_s, vb_s = mla_expand(ckv_all.reshape(bs * tk_s, KV_LORA), kr_all.reshape(bs * tk_s, ROPE_B),
                                wk, p_kr, wv)
        ob_s = attention(qb[n_p:].reshape(bs, ts, 1024), kb_s.reshape(bs, tk_s, 1024),
                         vb_s.reshape(bs, tk_s, 1024), q_off=past, tq=ts, tk=tk_s, diff=False,
                         name="mla_attn_s")
        ob = jnp.concatenate([ob_p.reshape(n_p, 1024), ob_s.reshape(n_s, 1024)], axis=0)

        alog_row = _misc_row(gdn_a_log[l], MISC_A0)
        dtb_row = _misc_row(gdn_dt_bias[l], MISC_A0)
        oc_p, s_p = gated_delta(slab, zero_conv, zero_state, gdn_conv[l], alog_row, dtb_row, gdn_norm[l],
                                row_blk0=0, n_streams=bp, n_chunks=tp // CHUNK)
        oc_s, s_s = gated_delta(slab, state_gdn_conv[l], state_gdn_s[l], gdn_conv[l], alog_row, dtb_row,
                                gdn_norm[l], row_blk0=n_p // CHUNK, n_streams=bs, n_chunks=ts // CHUNK)
        oc = jnp.concatenate([oc_p, oc_s], axis=0)
        cqkv = slab[:, COL_CQ:COL_CQ + QKV_C]
        conv_p = cqkv[:n_p].reshape(bp, tp, QKV_C)[:, tp - (CONV_W - 1):]
        conv_s = cqkv[n_p:].reshape(bs, ts, QKV_C)[:, ts - (CONV_W - 1):]

        merged = branch_merge(oa, ob, oc, w_branch[l].astype(BF16), gates)
        x = matmul(merged, w_out[l].astype(BF16), F32, residual=x, name="proj_out")

        w_router = jnp.concatenate([router_group[l], router_expert[l],
                                    jnp.zeros((D_MODEL, LANES - N_GROUPS - N_EXPERTS), F32)], axis=1).astype(BF16)
        b_router = jnp.concatenate([router_group_bias[l], router_expert_bias[l],
                                    jnp.zeros((LANES - N_GROUPS - N_EXPERTS,), F32)]).reshape(1, LANES)
        x = hier_moe_residual(x, norm_ffn[l], w_router, b_router, expert_w_gate[l].astype(BF16),
                              expert_w_up[l].astype(BF16), expert_w_down[l].astype(BF16))

        st_p.append((ka[:n_p].reshape(bp, tp, H_A, 2, DH_A), va[:n_p].reshape(bp, tp, H_A, VA),
                     ckv[:n_p].reshape(bp, tp, KV_LORA), kr[:n_p].reshape(bp, tp, ROPE_B), conv_p, s_p))
        st_s.append((ka[n_p:].reshape(bs, ts, H_A, 2, DH_A), va[n_p:].reshape(bs, ts, H_A, VA),
                     ckv[n_p:].reshape(bs, ts, KV_LORA), kr[n_p:].reshape(bs, ts, ROPE_B), conv_s, s_s))

    y = rmsnorm_rows(x, norm_final, F32)
    y_prompt = y[:n_p].reshape(bp, tp, D_MODEL)
    y_sample = y[n_p:].reshape(bs, ts, D_MODEL)

    def stk(sts, i):
        return jnp.stack([s[i] for s in sts])

    return (y_prompt, y_sample,
            stk(st_p, 0), stk(st_p, 1), stk(st_p, 2), stk(st_p, 3), stk(st_p, 4), stk(st_p, 5),
            stk(st_s, 0), stk(st_s, 1), stk(st_s, 2), stk(st_s, 3), stk(st_s, 4), stk(st_s, 5))
```

```python
import functools
import math

import jax
import jax.numpy as jnp
from jax import lax
from jax.experimental import pallas as pl
from jax.experimental.pallas import tpu as pltpu

F32 = jnp.float32
BF16 = jnp.bfloat16

D_MODEL = 2048
CHUNK = 64
CHUNK_SHIFT = 6
ROPE_THETA = 500000.0
EPS = 1e-6
N_BRANCH = 3
H_A = 8
DH_A = 64
VA = 2 * DH_A
ROT_A = DH_A // 4
H_B = 8
Q_LORA = 512
KV_LORA = 256
NOPE_B = 64
ROPE_B = 32
V_B = 128
H_C = 8
DK_C = 128
DV_C = 128
CONV_W = 4
QKV_C = 2 * H_C * DK_C + H_C * DV_C
BRANCH_W = H_A * VA
N_GROUPS = 4
EXPERTS_PER_GROUP = 4
N_EXPERTS = N_GROUPS * EXPERTS_PER_GROUP
D_EXPERT = 512

LANES = 128
VMEM_LIMIT_BYTES = 48 * 1024 * 1024
EXPERT_VMEM_LIMIT_BYTES = 56 * 1024 * 1024
NEG_BIG = -1e30

COL_AQ, COL_AK, COL_AV = 0, 1024, 2048
COL_DQ, COL_DKV, COL_MISC = 3072, 3584, 3840
COL_CQ, COL_CK, COL_CV, COL_CZ = 4096, 5120, 6144, 7168
SLAB_F = 8192
MISC_B0 = ROPE_B
MISC_A0 = ROPE_B + H_C

MOE_ROWS = 256
ATTN_TQ = 512
ATTN_TK = 512


def _cparams(*sem):
    return pltpu.CompilerParams(dimension_semantics=sem, vmem_limit_bytes=VMEM_LIMIT_BYTES)


def _sigmoid(x):
    return 1.0 / (1.0 + jnp.exp(-x))


def _silu(x):
    return x * _sigmoid(x)


def _rmsnorm_kernel(x_ref, w_ref, o_ref):
    x = x_ref[...]
    ms = jnp.mean(x * x, axis=-1, keepdims=True)
    o_ref[...] = (x * lax.rsqrt(ms + EPS) * w_ref[...]).astype(o_ref.dtype)


def rmsnorm_rows(x, w, out_dtype, tm=512):
    m, d = x.shape
    return pl.pallas_call(
        _rmsnorm_kernel,
        grid=(m // tm,),
        in_specs=[pl.BlockSpec((tm, d), lambda i: (i, 0)),
                  pl.BlockSpec((1, d), lambda i: (0, 0))],
        out_specs=pl.BlockSpec((tm, d), lambda i: (i, 0)),
        out_shape=jax.ShapeDtypeStruct((m, d), out_dtype),
        compiler_params=_cparams("parallel"),
        name="rmsnorm",
    )(x, w.reshape(1, d))


def _mm_kernel(a_ref, b_ref, o_ref, *, act):
    acc = jnp.dot(a_ref[...], b_ref[...], preferred_element_type=F32)
    if act == "sigmoid":
        acc = _sigmoid(acc)
    o_ref[...] = acc.astype(o_ref.dtype)


def _mm_res_kernel(a_ref, b_ref, r_ref, o_ref):
    acc = jnp.dot(a_ref[...], b_ref[...], preferred_element_type=F32)
    o_ref[...] = r_ref[...] + acc


def matmul(a, b, out_dtype, act=None, residual=None, tm=512, tn=1024, name="mm"):
    m, k = a.shape
    n = b.shape[1]
    in_specs = [pl.BlockSpec((tm, k), lambda i, j: (i, 0)),
                pl.BlockSpec((k, tn), lambda i, j: (0, j))]
    args = [a, b]
    if residual is None:
        body = functools.partial(_mm_kernel, act=act)
    else:
        body = _mm_res_kernel
        in_specs.append(pl.BlockSpec((tm, tn), lambda i, j: (i, j)))
        args.append(residual)
    return pl.pallas_call(
        body,
        grid=(m // tm, n // tn),
        in_specs=in_specs,
        out_specs=pl.BlockSpec((tm, tn), lambda i, j: (i, j)),
        out_shape=jax.ShapeDtypeStruct((m, n), out_dtype),
        compiler_params=_cparams("parallel", "parallel"),
        name=name,
    )(*args)


def _rope_tables(pos, period, offset, half):
    rot = 2 * half
    inv = ROPE_THETA ** (-jnp.arange(half, dtype=F32) * 2.0 / rot)
    ang = pos.astype(F32)[:, None] * inv[None, :]
    cos, sin = jnp.cos(ang), jnp.sin(ang)
    lane = jnp.arange(LANES)
    r = (lane % period) - offset
    first = (r >= 0) & (r < half)
    second = (r >= half) & (r < rot)
    idx = jnp.clip(jnp.where(second, r - half, r), 0, half - 1)
    cos_l, sin_l = cos[:, idx], sin[:, idx]
    c = jnp.where((first | second)[None, :], cos_l, 1.0)
    sa = jnp.where(first[None, :], -sin_l, 0.0)
    sb = jnp.where(second[None, :], sin_l, 0.0)
    return c, sa, sb


def _rope_lanes(x, c, sa, sb, half):
    return (x * c + pltpu.roll(x, LANES - half, axis=1) * sa
            + pltpu.roll(x, half, axis=1) * sb)


def _prep_kernel(aq_ref, ak_ref, av_ref, dq_ref, dkv_ref, misc_ref,
                 ca_ref, saa_ref, sba_ref, cb_ref, sab_ref, sbb_ref, ck_ref, sak_ref, sbk_ref,
                 qn_ref, wuq_ref, kvn_ref,
                 qa_o, ka_o, kabf_o, va_o, vabf_o, qb_o, ckv_o, kr_o):
    ca, saa, sba = ca_ref[...], saa_ref[...], sba_ref[...]
    for c in range(H_A):
        sl = slice(c * LANES, (c + 1) * LANES)
        q = _rope_lanes(aq_ref[:, sl], ca, saa, sba, ROT_A // 2)
        qa_o[:, sl] = (q * (DH_A ** -0.5)).astype(BF16)
        k = _rope_lanes(ak_ref[:, sl], ca, saa, sba, ROT_A // 2)
        ka_o[:, sl] = k
        kabf_o[:, sl] = k.astype(BF16)
    v = av_ref[...]
    va_o[...] = v
    vabf_o[...] = v.astype(BF16)

    x = dq_ref[...]
    cq = x * lax.rsqrt(jnp.mean(x * x, axis=-1, keepdims=True) + EPS) * qn_ref[...]
    qb = jnp.dot(cq.astype(BF16), wuq_ref[...], preferred_element_type=F32)
    cb, sab, sbb = cb_ref[...], sab_ref[...], sbb_ref[...]
    scale_b = (NOPE_B + ROPE_B) ** -0.5
    for c in range(H_B):
        sl = slice(c * LANES, (c + 1) * LANES)
        q = _rope_lanes(qb[:, sl], cb, sab, sbb, ROPE_B // 2)
        qb_o[:, sl] = (q * scale_b).astype(BF16)

    x = dkv_ref[...]
    ckv_o[...] = x * lax.rsqrt(jnp.mean(x * x, axis=-1, keepdims=True) + EPS) * kvn_ref[...]
    kr = _rope_lanes(misc_ref[...], ck_ref[...], sak_ref[...], sbk_ref[...], ROPE_B // 2)
    kr_o[...] = kr[:, :ROPE_B]


def mixer_prep(slab, tabs, q_norm, w_uq_pad, kv_norm, tm=256):
    m = slab.shape[0]

    def col(width, start):
        blk = start // width
        return pl.BlockSpec((tm, width), lambda i: (i, blk))

    def row(width):
        return pl.BlockSpec((tm, width), lambda i: (i, 0))

    def full(shape):
        return pl.BlockSpec(shape, lambda i: (0,) * len(shape))

    in_specs = [col(1024, COL_AQ), col(1024, COL_AK), col(1024, COL_AV),
                col(Q_LORA, COL_DQ), col(KV_LORA, COL_DKV), col(LANES, COL_MISC)]
    in_specs += [row(LANES)] * 9
    in_specs += [full((1, Q_LORA)), full((Q_LORA, 1024)), full((1, KV_LORA))]
    out_shape = [jax.ShapeDtypeStruct((m, 1024), BF16),
                 jax.ShapeDtypeStruct((m, 1024), F32),
                 jax.ShapeDtypeStruct((m, 1024), BF16),
                 jax.ShapeDtypeStruct((m, 1024), F32),
                 jax.ShapeDtypeStruct((m, 1024), BF16),
                 jax.ShapeDtypeStruct((m, 1024), BF16),
                 jax.ShapeDtypeStruct((m, KV_LORA), F32),
                 jax.ShapeDtypeStruct((m, ROPE_B), F32)]
    out_specs = [row(1024)] * 6 + [row(KV_LORA), row(ROPE_B)]
    return pl.pallas_call(
        _prep_kernel,
        grid=(m // tm,),
        in_specs=in_specs,
        out_specs=out_specs,
        out_shape=out_shape,
        compiler_params=_cparams("parallel"),
        name="mixer_prep",
    )(slab, slab, slab, slab, slab, slab, *tabs,
      q_norm.reshape(1, Q_LORA), w_uq_pad, kv_norm.reshape(1, KV_LORA))


def _expand_kernel(ckv_ref, kr_ref, wuk_ref, pk_ref, wuv_ref, kb_o, vb_o):
    ckv = ckv_ref[...].astype(BF16)
    kr = kr_ref[...].astype(BF16)
    kb = (jnp.dot(ckv, wuk_ref[...], preferred_element_type=F32)
          + jnp.dot(kr, pk_ref[...], preferred_element_type=F32))
    kb_o[...] = kb.astype(BF16)
    vb_o[...] = jnp.dot(ckv, wuv_ref[...], preferred_element_type=F32).astype(BF16)


def mla_expand(ckv, kr, w_uk_pad, p_kr, w_uv, tm=512):
    m = ckv.shape[0]
    return pl.pallas_call(
        _expand_kernel,
        grid=(m // tm,),
        in_specs=[pl.BlockSpec((tm, KV_LORA), lambda i: (i, 0)),
                  pl.BlockSpec((tm, ROPE_B), lambda i: (i, 0)),
                  pl.BlockSpec((KV_LORA, 1024), lambda i: (0, 0)),
                  pl.BlockSpec((ROPE_B, 1024), lambda i: (0, 0)),
                  pl.BlockSpec((KV_LORA, 1024), lambda i: (0, 0))],
        out_specs=[pl.BlockSpec((tm, 1024), lambda i: (i, 0))] * 2,
        out_shape=[jax.ShapeDtypeStruct((m, 1024), BF16)] * 2,
        compiler_params=_cparams("parallel"),
        name="mla_expand",
    )(ckv, kr, w_uk_pad, p_kr, w_uv)


def _attn_kernel(*refs, diff, tq, tk, q_off, nk, lam_init, aliased, group):
    q_ref, k_ref, v_ref = refs[:3]
    pos = 3
    if diff:
        lamp_ref, subln_ref = refs[3:5]
        pos = 5
    if aliased:
        pos += 1
    single = nk == 1
    if single:
        o_ref, bias_sc = refs[pos:pos + 2]
    else:
        o_ref, bias_sc, m_sc, l_sc, acc_sc = refs[pos:pos + 5]
    n_maps = 2 if diff else 1
    maps = [(h, mp) for h in range(H_A) for mp in range(n_maps)]
    i = pl.program_id(1)
    j = pl.program_id(2)

    def write_head(h, outs):
        if diff:
            lp = lamp_ref[...]
            lam = (jnp.exp(jnp.sum(lp[0:1] * lp[1:2], axis=-1, keepdims=True))
                   - jnp.exp(jnp.sum(lp[2:3] * lp[3:4], axis=-1, keepdims=True)) + lam_init)
            o = outs[0] - lam * outs[1]
            o = o * lax.rsqrt(jnp.mean(o * o, axis=-1, keepdims=True) + EPS) * subln_ref[...]
            o = o * (1.0 - lam_init)
        else:
            o = outs[0]
        o_ref[:, h * LANES:(h + 1) * LANES] = o.astype(o_ref.dtype)

    if not single:
        @pl.when(j == 0)
        def _():
            m_sc[...] = jnp.full(m_sc.shape, NEG_BIG, F32)
            l_sc[...] = jnp.zeros(l_sc.shape, F32)
            acc_sc[...] = jnp.zeros(acc_sc.shape, F32)

    q_chunk_max = (q_off + i * tq + tq - 1) // CHUNK

    @pl.when((j * tk) // CHUNK <= q_chunk_max)
    def _():
        q_chunk = lax.shift_right_logical(q_off + i * tq + lax.broadcasted_iota(jnp.int32, (tq, 1), 0),
                                          CHUNK_SHIFT)
        k_chunk = lax.shift_right_logical(j * tk + lax.broadcasted_iota(jnp.int32, (1, tk), 1), CHUNK_SHIFT)
        bias_sc[...] = jnp.where(q_chunk >= k_chunk, 0.0, NEG_BIG)
        lane = lax.broadcasted_iota(jnp.int32, (1, LANES), 1)

        def scores(h, mp):
            hs = slice(h * LANES, (h + 1) * LANES)
            q = q_ref[:, hs]
            if diff:
                in_map = (lane < DH_A) if mp == 0 else (lane >= DH_A)
                q = jnp.where(in_map, q, jnp.zeros_like(q))
            return lax.dot_general(q, k_ref[:, hs], (((1,), (1,)), ((), ())),
                                   preferred_element_type=F32) + bias_sc[...]

        def pv(p, h):
            return jnp.dot(p.astype(BF16), v_ref[:, h * LANES:(h + 1) * LANES], preferred_element_type=F32)

        for g0 in range(0, len(maps), group):
            grp = maps[g0:g0 + group]
            s = [scores(h, mp) for h, mp in grp]
            m_cur = [jnp.max(x, axis=-1, keepdims=True) for x in s]
            if single:
                p = [jnp.exp(x - m) for x, m in zip(s, m_cur)]
                l = [jnp.sum(x, axis=-1, keepdims=True) for x in p]
                outs = [pv(x, h) / d for x, d, (h, _) in zip(p, l, grp)]
                for t in range(0, len(grp), n_maps):
                    write_head(grp[t][0], outs[t:t + n_maps])
            else:
                ids = [h * n_maps + mp for h, mp in grp]
                m_prev = [m_sc[t] for t in ids]
                m_new = [jnp.maximum(a, b) for a, b in zip(m_prev, m_cur)]
                p = [jnp.exp(x - jnp.concatenate([m] * (tk // LANES), axis=1)) for x, m in zip(s, m_new)]
                alpha = [jnp.exp(a - b) for a, b in zip(m_prev, m_new)]
                for t, a, x, m, (h, _) in zip(ids, alpha, p, m_new, grp):
                    l_sc[t] = a * l_sc[t] + jnp.sum(x, axis=-1, keepdims=True)
                    acc_sc[t] = a * acc_sc[t] + pv(x, h)
                    m_sc[t] = m

    if not single:
        @pl.when(j == nk - 1)
        def _():
            for h in range(H_A):
                write_head(h, [acc_sc[h * n_maps + mp] / l_sc[h * n_maps + mp] for mp in range(n_maps)])


def attention(q, k, v, out_prev, *, n_streams, q_row0, t_q, t_k, q_off, tq, tk, diff,
              lamp=None, subln=None, lam_init=0.0, group=2, name="attn"):
    nq, nk = t_q // tq, t_k // tk
    qb0 = q_row0 // tq
    width = H_A * LANES

    def q_map(b, i, j):
        return (qb0 + b * nq + i, 0)

    def kv_map(b, i, j):
        last = ((q_off + i * tq + tq - 1) // CHUNK * CHUNK) // tk
        return (b * nk + jnp.minimum(j, last), 0)

    in_specs = [pl.BlockSpec((tq, width), q_map),
                pl.BlockSpec((tk, width), kv_map),
                pl.BlockSpec((tk, width), kv_map)]
    args = [q, k, v]
    n_maps = 1
    if diff:
        n_maps = 2
        in_specs += [pl.BlockSpec((4, DH_A), lambda b, i, j: (0, 0)),
                     pl.BlockSpec((1, VA), lambda b, i, j: (0, 0))]
        args += [lamp, subln.reshape(1, VA)]
    aliases = {}
    if out_prev is not None:
        aliases = {len(args): 0}
        in_specs.append(pl.BlockSpec(memory_space=pl.ANY))
        args.append(out_prev)
    return pl.pallas_call(
        functools.partial(_attn_kernel, diff=diff, tq=tq, tk=tk, q_off=q_off, nk=nk, lam_init=lam_init,
                          aliased=out_prev is not None, group=group),
        grid=(n_streams, nq, nk),
        in_specs=in_specs,
        out_specs=pl.BlockSpec((tq, width), q_map),
        out_shape=jax.ShapeDtypeStruct((q.shape[0], width), BF16),
        scratch_shapes=[pltpu.VMEM((tq, tk), F32)] + ([] if nk == 1 else
                                                      [pltpu.VMEM((H_A * n_maps, tq, LANES), F32)] * 3),
        input_output_aliases=aliases,
        compiler_params=_cparams("parallel", "parallel", "arbitrary"),
        name=name,
    )(*args)


def _mm_nt(a, b):
    return lax.dot_general(a, b, (((1,), (1,)), ((), ())), preferred_element_type=F32)


def _mm_nn(a, b):
    return jnp.dot(a, b, preferred_element_type=F32)


def _mm_tn(a, b):
    return lax.dot_general(a, b, (((0,), (0,)), ((), ())), preferred_element_type=F32)


def _bf(xs):
    return [x.astype(BF16) for x in xs]


def _shift_rows(x, tail, s):
    xs = pltpu.roll(x, s, axis=0)
    ts = pltpu.roll(tail, s, axis=0)
    row = lax.broadcasted_iota(jnp.int32, (8, 1), 0)
    top = jnp.where(row < s, ts, xs[0:8])
    return jnp.concatenate([top, xs[8:]], axis=0)


def _gdn_kernel(*refs, n_chunks, aliased):
    (cq_ref, ck_ref, cv_ref, cz_ref, misc_ref, convp_ref, s0_ref,
     kern_ref, alog_ref, dtb_ref, norm_ref) = refs[:11]
    o_ref, s_o, s_sc, tail_sc = refs[(12 if aliased else 11):]
    n = pl.program_id(1)
    L = CHUNK
    heads = range(H_C)

    @pl.when(n == 0)
    def _():
        s_sc[...] = s0_ref[0]
        tail_sc[...] = jnp.zeros(tail_sc.shape, F32)
        tail_sc[8 - (CONV_W - 1):8, :] = convp_ref[0]

    conv = []
    for part, ref in enumerate((cq_ref, ck_ref, cv_ref)):
        cs = slice(part * 1024, (part + 1) * 1024)
        x = ref[...]
        tail = tail_sc[:, cs]
        acc = x * kern_ref[CONV_W - 1:CONV_W, cs]
        for s in range(1, CONV_W):
            acc = acc + _shift_rows(x, tail, s) * kern_ref[CONV_W - 1 - s:CONV_W - s, cs]
        conv.append(_silu(acc))
        tail_sc[:, cs] = x[L - 8:L]
    qc, kc, vc = conv

    misc = misc_ref[...]
    beta_all = _sigmoid(misc)
    za = misc + dtb_ref[...]
    softplus = jnp.maximum(za, 0.0) + jnp.log(1.0 + jnp.exp(-jnp.abs(za)))
    g_all = -jnp.exp(alog_ref[...]) * softplus
    row = lax.broadcasted_iota(jnp.int32, (L, 1), 0)
    gcum_all = g_all
    for s in (1, 2, 4, 8, 16, 32):
        gcum_all = gcum_all + jnp.where(row >= s, pltpu.roll(gcum_all, s, axis=0), 0.0)
    lane = lax.broadcasted_iota(jnp.int32, (1, LANES), 1)
    is_beta = (lane >= MISC_B0) & (lane < MISC_B0 + H_C)
    packed = jnp.where(is_beta, beta_all, gcum_all)
    packed_t = jnp.concatenate([packed, jnp.zeros((LANES - L, LANES), F32)], axis=0).T

    ri = lax.broadcasted_iota(jnp.int32, (L, L), 0)
    ci = lax.broadcasted_iota(jnp.int32, (L, L), 1)
    incl = ri >= ci
    strict = ri > ci
    diag16 = (ri // 16) == (ci // 16)

    qn, kn, v_h, beta, gcum, decay, gam = [], [], [], [], [], [], []
    for h in heads:
        hs = slice(h * LANES, (h + 1) * LANES)
        qh, kh = qc[:, hs], kc[:, hs]
        qn.append(qh * lax.rsqrt(jnp.sum(qh * qh, axis=-1, keepdims=True) + EPS) * (DK_C ** -0.5))
        kn.append(kh * lax.rsqrt(jnp.sum(kh * kh, axis=-1, keepdims=True) + EPS))
        v_h.append(vc[:, hs])
        beta.append(beta_all[:, MISC_B0 + h:MISC_B0 + h + 1])
        gc = gcum_all[:, MISC_A0 + h:MISC_A0 + h + 1]
        gcum.append(gc)
        seg = gc - packed_t[MISC_A0 + h:MISC_A0 + h + 1, 0:L]
        decay.append(jnp.where(incl, jnp.exp(jnp.where(incl, seg, 0.0)), 0.0))
        gam.append(jnp.exp(gc))
    q16, k16 = _bf(qn), _bf(kn)
    kk = [_mm_nt(k16[h], k16[h]) for h in heads]
    pq = [_mm_nt(q16[h], k16[h]) * decay[h] for h in heads]
    a_mat = [jnp.where(strict, beta[h] * kk[h] * decay[h], 0.0) for h in heads]

    a_d = [jnp.where(diag16, a_mat[h], 0.0) for h in heads]
    a_lo = [a_mat[h] - a_d[h] for h in heads]
    ad16 = _bf(a_d)
    a2 = [_mm_nn(ad16[h], ad16[h]) for h in heads]
    a2_16 = _bf(a2)
    a4 = [_mm_nn(a2_16[h], a2_16[h]) for h in heads]
    a4_16 = _bf(a4)
    a8 = [_mm_nn(a4_16[h], a4_16[h]) for h in heads]
    f12 = [a2[h] - a_d[h] - _mm_nn(ad16[h], a2_16[h]) for h in heads]
    a8_16 = _bf(a8)
    f34 = [a4[h] + a8[h] + _mm_nn(a4_16[h], a8_16[h]) for h in heads]
    f12_16, f34_16 = _bf(f12), _bf(f34)
    g = [f12[h] + f34[h] + _mm_nn(f12_16[h], f34_16[h]) for h in heads]
    g16, alo16 = _bf(g), _bf(a_lo)
    nm = [a_lo[h] + _mm_nn(g16[h], alo16[h]) for h in heads]
    n16 = _bf(nm)
    n2 = [_mm_nn(n16[h], n16[h]) for h in heads]
    n2_16 = _bf(n2)
    hm = [n2[h] - nm[h] - _mm_nn(n16[h], n2_16[h]) for h in heads]
    hm16 = _bf(hm)
    tm1 = [hm[h] + g[h] + _mm_nn(hm16[h], g16[h]) for h in heads]
    tm1_16 = _bf(tm1)
    rhs = [jnp.concatenate([v_h[h] * beta[h], kn[h] * (beta[h] * gam[h])], axis=-1) for h in heads]
    rhs16 = _bf(rhs)
    sol = [rhs[h] + _mm_nn(tm1_16[h], rhs16[h]) for h in heads]

    s_prev = [s_sc[h] for h in heads]
    s16 = _bf(s_prev)
    w16 = _bf([sol[h][:, DV_C:] for h in heads])
    u = [sol[h][:, :DV_C] - _mm_nt(w16[h], s16[h]) for h in heads]
    u16 = _bf(u)
    pq16 = _bf(pq)
    o = [gam[h] * _mm_nt(q16[h], s16[h]) + _mm_nn(pq16[h], u16[h]) for h in heads]
    for h in heads:
        gcum_end = gcum[h][L - 1:L, :]
        ut = (u[h] * jnp.exp(gcum_end - gcum[h])).astype(BF16)
        s_sc[h] = jnp.exp(gcum_end) * s_prev[h] + _mm_tn(ut, k16[h])
    for h in heads:
        hs = slice(h * LANES, (h + 1) * LANES)
        oh = o[h] * lax.rsqrt(jnp.mean(o[h] * o[h], axis=-1, keepdims=True) + EPS) * norm_ref[...]
        o_ref[:, hs] = (oh * _silu(cz_ref[:, hs])).astype(o_ref.dtype)

    @pl.when(n == n_chunks - 1)
    def _():
        s_o[0] = s_sc[...]


def gated_delta(slab, conv_prev, s0, kern, alog_row, dtb_row, norm_w, out_prev, *, row_blk0, n_streams, n_chunks):
    def rows(b, n):
        return row_blk0 + b * n_chunks + n

    def col(start):
        blk = start // 1024
        return pl.BlockSpec((CHUNK, 1024), lambda b, n: (rows(b, n), blk))

    def full(shape):
        return pl.BlockSpec(shape, lambda b, n: (0,) * len(shape))

    in_specs = [col(COL_CQ), col(COL_CK), col(COL_CV), col(COL_CZ),
                pl.BlockSpec((CHUNK, LANES), lambda b, n: (rows(b, n), COL_MISC // LANES)),
                pl.BlockSpec((1, CONV_W - 1, QKV_C), lambda b, n: (b, 0, 0)),
                pl.BlockSpec((1, H_C, DV_C, DK_C), lambda b, n: (b, 0, 0, 0)),
                full((CONV_W, QKV_C)), full((1, LANES)), full((1, LANES)), full((1, DV_C))]
    args = [slab, slab, slab, slab, slab, conv_prev, s0, kern, alog_row, dtb_row, norm_w.reshape(1, DV_C)]
    aliases = {}
    if out_prev is not None:
        aliases = {len(args): 0}
        in_specs.append(pl.BlockSpec(memory_space=pl.ANY))
        args.append(out_prev)
    return pl.pallas_call(
        functools.partial(_gdn_kernel, n_chunks=n_chunks, aliased=out_prev is not None),
        grid=(n_streams, n_chunks),
        in_specs=in_specs,
        out_specs=[pl.BlockSpec((CHUNK, 1024), lambda b, n: (rows(b, n), 0)),
                   pl.BlockSpec((1, H_C, DV_C, DK_C), lambda b, n: (b, 0, 0, 0))],
        out_shape=[jax.ShapeDtypeStruct((slab.shape[0], 1024), BF16),
                   jax.ShapeDtypeStruct((n_streams, H_C, DV_C, DK_C), F32)],
        scratch_shapes=[pltpu.VMEM((H_C, DV_C, DK_C), F32),
                        pltpu.VMEM((8, QKV_C), F32)],
        input_output_aliases=aliases,
        compiler_params=_cparams("parallel", "arbitrary"),
        name="gated_delta",
    )(*args)


def _merge_kernel(oa_ref, ob_ref, oc_ref, wb_ref, ga_ref, gb_ref, gc_ref, o_ref):
    acc = jnp.dot(oa_ref[...], wb_ref[0], preferred_element_type=F32) * ga_ref[...].astype(F32)
    acc = acc + jnp.dot(ob_ref[...], wb_ref[1], preferred_element_type=F32) * gb_ref[...].astype(F32)
    acc = acc + jnp.dot(oc_ref[...], wb_ref[2], preferred_element_type=F32) * gc_ref[...].astype(F32)
    o_ref[...] = acc.astype(o_ref.dtype)


def branch_merge(oa, ob, oc, w_branch, gates, tm=512, tn=1024):
    m = oa.shape[0]
    nj = D_MODEL // tn

    def gate_spec(nb):
        return pl.BlockSpec((tm, tn), lambda i, j: (i, nb * nj + j))

    return pl.pallas_call(
        _merge_kernel,
        grid=(m // tm, nj),
        in_specs=[pl.BlockSpec((tm, BRANCH_W), lambda i, j: (i, 0))] * 3
                 + [pl.BlockSpec((N_BRANCH, BRANCH_W, tn), lambda i, j: (0, 0, j))]
                 + [gate_spec(0), gate_spec(1), gate_spec(2)],
        out_specs=pl.BlockSpec((tm, tn), lambda i, j: (i, j)),
        out_shape=jax.ShapeDtypeStruct((m, D_MODEL), BF16),
        compiler_params=_cparams("parallel", "parallel"),
        name="branch_merge",
    )(oa, ob, oc, w_branch, gates, gates, gates)


def _router_kernel(x_ref, nw_ref, wr_ref, br_ref, h_o, info_o):
    x = x_ref[...]
    h = (x * lax.rsqrt(jnp.mean(x * x, axis=-1, keepdims=True) + EPS) * nw_ref[...])
    hb = h.astype(BF16)
    h_o[...] = h
    logits = jnp.dot(hb, wr_ref[...], preferred_element_type=F32) + br_ref[...]
    lane = lax.broadcasted_iota(jnp.int32, logits.shape, 1)
    is_g = lane < N_GROUPS
    lg = jnp.where(is_g, logits, NEG_BIG)
    eg = jnp.where(is_g, jnp.exp(lg - jnp.max(lg, axis=-1, keepdims=True)), 0.0)
    gp = eg / jnp.sum(eg, axis=-1, keepdims=True)
    pg = jnp.max(gp, axis=-1, keepdims=True)
    grp = jnp.min(jnp.where(is_g & (gp == pg), lane, LANES), axis=-1, keepdims=True)
    lo = N_GROUPS + grp * EXPERTS_PER_GROUP
    is_e = (lane >= lo) & (lane < lo + EXPERTS_PER_GROUP)
    le = jnp.where(is_e, logits, NEG_BIG)
    ee = jnp.where(is_e, jnp.exp(le - jnp.max(le, axis=-1, keepdims=True)), 0.0)
    ep = ee / jnp.sum(ee, axis=-1, keepdims=True)
    v1 = jnp.max(jnp.where(is_e, ep, -1.0), axis=-1, keepdims=True)
    i1 = jnp.min(jnp.where(is_e & (ep == v1), lane, LANES), axis=-1, keepdims=True)
    rest = is_e & (lane != i1)
    v2 = jnp.max(jnp.where(rest, ep, -1.0), axis=-1, keepdims=True)
    i2 = jnp.min(jnp.where(rest & (ep == v2), lane, LANES), axis=-1, keepdims=True)
    den = v1 + v2
    w1 = pg * v1 / den
    w2 = pg * v2 / den
    info = jnp.where(lane == 0, (i1 - N_GROUPS).astype(F32), 0.0)
    info = jnp.where(lane == 1, (i2 - N_GROUPS).astype(F32), info)
    info = jnp.where(lane == 2, w1, info)
    info = jnp.where(lane == 3, w2, info)
    info_o[...] = info


def moe_router(x, norm_w, w_router, b_router, tm=512):
    m = x.shape[0]
    return pl.pallas_call(
        _router_kernel,
        grid=(m // tm,),
        in_specs=[pl.BlockSpec((tm, D_MODEL), lambda i: (i, 0)),
                  pl.BlockSpec((1, D_MODEL), lambda i: (0, 0)),
                  pl.BlockSpec((D_MODEL, LANES), lambda i: (0, 0)),
                  pl.BlockSpec((1, LANES), lambda i: (0, 0))],
        out_specs=[pl.BlockSpec((tm, D_MODEL), lambda i: (i, 0)),
                   pl.BlockSpec((tm, LANES), lambda i: (i, 0))],
        out_shape=[jax.ShapeDtypeStruct((m, D_MODEL), F32),
                   jax.ShapeDtypeStruct((m, LANES), F32)],
        compiler_params=_cparams("parallel"),
        name="moe_router",
    )(x, norm_w.reshape(1, D_MODEL), w_router, b_router)


def _gather_kernel(*refs, n_idx, rows, has_res):
    idx_refs = refs[:n_idx]
    src_ref = refs[n_idx]
    pos = n_idx + 1
    res_ref = None
    if has_res:
        res_ref = refs[pos]
        pos += 1
    o_ref = refs[pos]
    bufs = refs[pos + 1:pos + 1 + n_idx]
    sem = refs[pos + 1 + n_idx]

    def copy(k, r):
        return pltpu.make_async_copy(src_ref.at[pl.ds(idx_refs[k][0, 0, r], 1), :],
                                     bufs[k].at[pl.ds(r, 1), :], sem.at[k])

    def start(r, c):
        for k in range(n_idx):
            copy(k, r).start()
        return c

    lax.fori_loop(0, rows, start, 0)

    def wait(r, c):
        for k in range(n_idx):
            copy(k, r).wait()
        return c

    lax.fori_loop(0, rows, wait, 0)
    acc = bufs[0][...]
    for k in range(1, n_idx):
        acc = acc + bufs[k][...]
    if has_res:
        acc = res_ref[...] + acc
    o_ref[...] = acc


def gather_rows(src, idxs, residual=None, rows=MOE_ROWS):
    n_out = idxs[0].shape[0]
    d = src.shape[1]
    nb = n_out // rows
    n_idx = len(idxs)
    idx3 = [ix.reshape(nb, 1, rows) for ix in idxs]
    in_specs = [pl.BlockSpec((1, 1, rows), lambda i: (i, 0, 0), memory_space=pltpu.SMEM)] * n_idx
    in_specs.append(pl.BlockSpec(memory_space=pl.ANY))
    args = idx3 + [src]
    if residual is not None:
        in_specs.append(pl.BlockSpec((rows, d), lambda i: (i, 0)))
        args.append(residual)
    return pl.pallas_call(
        functools.partial(_gather_kernel, n_idx=n_idx, rows=rows, has_res=residual is not None),
        grid=(nb,),
        in_specs=in_specs,
        out_specs=pl.BlockSpec((rows, d), lambda i: (i, 0)),
        out_shape=jax.ShapeDtypeStruct((n_out, d), src.dtype),
        scratch_shapes=[pltpu.VMEM((rows, d), src.dtype)] * n_idx + [pltpu.SemaphoreType.DMA((n_idx,))],
        compiler_params=_cparams("arbitrary"),
        name="gather_rows",
    )(*args)


def _expert_kernel(be_ref, nu_ref, x_ref, rw_ref, wg_ref, wu_ref, wd_ref, o_ref, wg_sc, wu_sc, wd_sc):
    b = pl.program_id(0)
    prev = be_ref[jnp.maximum(b - 1, 0)]

    @pl.when((b == 0) | (be_ref[b] != prev))
    def _():
        wg_sc[...] = wg_ref[0].astype(BF16)
        wu_sc[...] = wu_ref[0].astype(BF16)
        wd_sc[...] = wd_ref[0].astype(BF16)

    @pl.when(b < nu_ref[0])
    def _():
        x = x_ref[...].astype(BF16)
        g = jnp.dot(x, wg_sc[...], preferred_element_type=F32)
        u = jnp.dot(x, wu_sc[...], preferred_element_type=F32)
        hid = _silu(g) * u * rw_ref[...]
        o_ref[...] = jnp.dot(hid.astype(BF16), wd_sc[...], preferred_element_type=F32)

    @pl.when(b >= nu_ref[0])
    def _():
        o_ref[...] = jnp.zeros(o_ref.shape, F32)


def routed_experts(xs, row_w, block_expert, n_used, w_gate, w_up, w_down, rows=MOE_ROWS):
    n_rows = xs.shape[0]
    nb = n_rows // rows
    grid_spec = pltpu.PrefetchScalarGridSpec(
        num_scalar_prefetch=2,
        grid=(nb,),
        in_specs=[pl.BlockSpec((rows, D_MODEL), lambda b, be, nu: (b, 0)),
                  pl.BlockSpec((rows, 1), lambda b, be, nu: (b, 0)),
                  pl.BlockSpec((1, D_MODEL, D_EXPERT), lambda b, be, nu: (be[b], 0, 0)),
                  pl.BlockSpec((1, D_MODEL, D_EXPERT), lambda b, be, nu: (be[b], 0, 0)),
                  pl.BlockSpec((1, D_EXPERT, D_MODEL), lambda b, be, nu: (be[b], 0, 0))],
        out_specs=pl.BlockSpec((rows, D_MODEL), lambda b, be, nu: (b, 0)),
        scratch_shapes=[pltpu.VMEM((D_MODEL, D_EXPERT), BF16),
                        pltpu.VMEM((D_MODEL, D_EXPERT), BF16),
                        pltpu.VMEM((D_EXPERT, D_MODEL), BF16)],
    )
    return pl.pallas_call(
        _expert_kernel,
        grid_spec=grid_spec,
        out_shape=jax.ShapeDtypeStruct((n_rows, D_MODEL), F32),
        compiler_params=pltpu.CompilerParams(dimension_semantics=("arbitrary",),
                                             vmem_limit_bytes=EXPERT_VMEM_LIMIT_BYTES),
        name="routed_experts",
    )(block_expert, n_used, xs, row_w, w_gate, w_up, w_down)


def _routing_plan(e1, e2, w1, w2, rows):
    n = e1.shape[0]
    n_slots = 2 * n + N_EXPERTS * rows
    oh1 = jax.nn.one_hot(e1, N_EXPERTS, dtype=jnp.int32)
    oh2 = jax.nn.one_hot(e2, N_EXPERTS, dtype=jnp.int32)
    both = oh1 + oh2
    before = jnp.cumsum(both, axis=0) - both
    rank1 = jnp.sum(before * oh1, axis=1)
    rank2 = jnp.sum(before * oh2, axis=1)
    count = jnp.sum(both, axis=0)
    padded = (count + rows - 1) // rows * rows
    ends = jnp.cumsum(padded)
    starts = ends - padded
    dest1 = starts[e1] + rank1
    dest2 = starts[e2] + rank2
    tok = jnp.arange(n, dtype=jnp.int32)
    row_token = jnp.zeros((n_slots,), jnp.int32).at[dest1].set(tok).at[dest2].set(tok)
    row_w = jnp.zeros((n_slots,), F32).at[dest1].set(w1).at[dest2].set(w2)
    blk_start = jnp.arange(n_slots // rows, dtype=jnp.int32) * rows
    block_expert = jnp.minimum(jnp.searchsorted(ends, blk_start, side="right"), N_EXPERTS - 1).astype(jnp.int32)
    n_used = (ends[-1] // rows).astype(jnp.int32).reshape(1)
    return row_token, row_w.reshape(n_slots, 1), block_expert, n_used, dest1.astype(jnp.int32), dest2.astype(jnp.int32)


def hier_moe_residual(x, norm_w, w_router, b_router, w_gate, w_up, w_down):
    h, info = moe_router(x, norm_w, w_router, b_router)
    e1 = info[:, 0].astype(jnp.int32)
    e2 = info[:, 1].astype(jnp.int32)
    row_token, row_w, block_expert, n_used, dest1, dest2 = _routing_plan(e1, e2, info[:, 2], info[:, 3], MOE_ROWS)
    xs = gather_rows(h, [row_token])
    ys = routed_experts(xs, row_w, block_expert, n_used, w_gate, w_up, w_down)
    return gather_rows(ys, [dest1, dest2], residual=x)


def _pack_w_in(w):
    sizes = (1024, 1024, 1024, Q_LORA, KV_LORA, ROPE_B, QKV_C, 1024, H_C, H_C, N_BRANCH * D_MODEL)
    offs = [0]
    for s in sizes:
        offs.append(offs[-1] + s)
    a_q, a_k, a_v, b_dq, b_dkv, b_kr, c_qkv, c_z, c_b, c_a, g_in = (
        w[:, offs[i]:offs[i + 1]] for i in range(len(sizes)))
    zeros = jnp.zeros((D_MODEL, 2 * LANES - ROPE_B - 2 * H_C), w.dtype)
    slab = jnp.concatenate([a_q, a_k, a_v, b_dq, b_dkv, b_kr, c_b, c_a, zeros, c_qkv, c_z], axis=1)
    return slab.astype(BF16), g_in.astype(BF16)


def _pack_mla(w_uq, w_ukv):
    wq = w_uq.reshape(Q_LORA, H_B, NOPE_B + ROPE_B)
    wq = jnp.pad(wq, ((0, 0), (0, 0), (0, LANES - NOPE_B - ROPE_B))).reshape(Q_LORA, H_B * LANES)
    wkv = w_ukv.reshape(KV_LORA, H_B, NOPE_B + V_B)
    wk = jnp.pad(wkv[:, :, :NOPE_B], ((0, 0), (0, 0), (0, LANES - NOPE_B))).reshape(KV_LORA, H_B * LANES)
    wv = wkv[:, :, NOPE_B:].reshape(KV_LORA, H_B * V_B)
    place = jnp.zeros((ROPE_B, H_B, LANES), F32)
    place = place.at[jnp.arange(ROPE_B), :, NOPE_B + jnp.arange(ROPE_B)].set(1.0)
    return wq.astype(BF16), wk.astype(BF16), place.reshape(ROPE_B, H_B * LANES).astype(BF16), wv.astype(BF16)


def _misc_row(vals, lane0):
    return jnp.zeros((1, LANES), F32).at[0, lane0:lane0 + H_C].set(vals.astype(F32))


def kernel(x_prompt, x_sample, cache_diff_k, cache_diff_v, cache_mla_ckv, cache_mla_krope, state_gdn_conv, state_gdn_s, norm_mix, w_in, diff_lambda, diff_subln, mla_q_norm, mla_w_uq, mla_kv_norm, mla_w_ukv, gdn_conv, gdn_a_log, gdn_dt_bias, gdn_norm, w_branch, w_out, norm_ffn, router_group, router_group_bias, router_expert, router_expert_bias, expert_w_gate, expert_w_up, expert_w_down, norm_final):
    bp, tp, _ = x_prompt.shape
    bs, ts, _ = x_sample.shape
    depth = w_in.shape[0]
    past = cache_diff_k.shape[2]
    n_p, n_s = bp * tp, bs * ts
    tk_s = past + ts

    x = jnp.concatenate([x_prompt.reshape(n_p, D_MODEL), x_sample.reshape(n_s, D_MODEL)], axis=0)
    pos = jnp.concatenate([jnp.tile(jnp.arange(tp), bp), jnp.tile(past + jnp.arange(ts), bs)])
    tabs = (_rope_tables(pos, DH_A, 0, ROT_A // 2)
            + _rope_tables(pos, LANES, NOPE_B, ROPE_B // 2)
            + _rope_tables(pos, LANES, 0, ROPE_B // 2))

    zero_conv = jnp.zeros((bp, CONV_W - 1, QKV_C), F32)
    zero_state = jnp.zeros((bp, H_C, DV_C, DK_C), F32)
    zero_branch = jnp.zeros((n_p + n_s, BRANCH_W), BF16)
    st_p, st_s = [], []
    for l in range(depth):
        lam_init = 0.8 - 0.6 * math.exp(-0.3 * l)
        w_slab, w_gates = _pack_w_in(w_in[l])
        wq, wk, p_kr, wv = _pack_mla(mla_w_uq[l], mla_w_ukv[l])

        h = rmsnorm_rows(x, norm_mix[l], BF16)
        slab = matmul(h, w_slab, F32, name="proj_in")
        gates = matmul(h, w_gates, BF16, act="sigmoid", name="proj_gates")
        qa, ka, ka_bf, va, va_bf, qb, ckv, kr = mixer_prep(slab, tabs, mla_q_norm[l], wq, mla_kv_norm[l])

        diff_kw = dict(diff=True, lamp=diff_lambda[l], subln=diff_subln[l], lam_init=lam_init)
        oa = attention(qa, ka_bf, va_bf, zero_branch, n_streams=bp, q_row0=0, t_q=tp, t_k=tp, q_off=0,
                       tq=ATTN_TQ, tk=ATTN_TK, name="diff_attn_p", **diff_kw)
        ka_all = jnp.concatenate([cache_diff_k[l].reshape(bs, past, 1024).astype(BF16),
                                  ka_bf[n_p:].reshape(bs, ts, 1024)], axis=1).reshape(bs * tk_s, 1024)
        va_all = jnp.concatenate([cache_diff_v[l].reshape(bs, past, 1024).astype(BF16),
                                  va_bf[n_p:].reshape(bs, ts, 1024)], axis=1).reshape(bs * tk_s, 1024)
        oa = attention(qa, ka_all, va_all, oa, n_streams=bs, q_row0=n_p, t_q=ts, t_k=tk_s, q_off=past,
                       tq=ts, tk=tk_s, name="diff_attn_s", **diff_kw)

        kb, vb = mla_expand(ckv, kr, wk, p_kr, wv)
        ob = attention(qb, kb, vb, zero_branch, n_streams=bp, q_row0=0, t_q=tp, t_k=tp, q_off=0,
                       tq=ATTN_TQ, tk=ATTN_TK, diff=False, name="mla_attn_p")
        ckv_all = jnp.concatenate([cache_mla_ckv[l], ckv[n_p:].reshape(bs, ts, KV_LORA)], axis=1)
        kr_all = jnp.concatenate([cache_mla_krope[l], kr[n_p:].reshape(bs, ts, ROPE_B)], axis=1)
        kb_s, vb_s = mla_expand(ckv_all.reshape(bs * tk_s, KV_LORA), kr_all.reshape(bs * tk_s, ROPE_B),
                                wk, p_kr, wv)
        ob = attention(qb, kb_s, vb_s, ob, n_streams=bs, q_row0=n_p, t_q=ts, t_k=tk_s, q_off=past,
                       tq=ts, tk=tk_s, diff=False, name="mla_attn_s")

        alog_row = _misc_row(gdn_a_log[l], MISC_A0)
        dtb_row = _misc_row(gdn_dt_bias[l], MISC_A0)
        oc, s_p = gated_delta(slab, zero_conv, zero_state, gdn_conv[l], alog_row, dtb_row, gdn_norm[l], zero_branch,
                              row_blk0=0, n_streams=bp, n_chunks=tp // CHUNK)
        oc, s_s = gated_delta(slab, state_gdn_conv[l], state_gdn_s[l], gdn_conv[l], alog_row, dtb_row,
                              gdn_norm[l], oc, row_blk0=n_p // CHUNK, n_streams=bs, n_chunks=ts // CHUNK)
        cqkv = slab[:, COL_CQ:COL_CQ + QKV_C]
        conv_p = cqkv[:n_p].reshape(bp, tp, QKV_C)[:, tp - (CONV_W - 1):]
        conv_s = cqkv[n_p:].reshape(bs, ts, QKV_C)[:, ts - (CONV_W - 1):]

        merged = branch_merge(oa, ob, oc, w_branch[l].astype(BF16), gates)
        x = matmul(merged, w_out[l].astype(BF16), F32, residual=x, name="proj_out")

        w_router = jnp.concatenate([router_group[l], router_expert[l],
                                    jnp.zeros((D_MODEL, LANES - N_GROUPS - N_EXPERTS), F32)], axis=1).astype(BF16)
        b_router = jnp.concatenate([router_group_bias[l], router_expert_bias[l],
                                    jnp.zeros((LANES - N_GROUPS - N_EXPERTS,), F32)]).reshape(1, LANES)
        x = hier_moe_residual(x, norm_ffn[l], w_router, b_router, expert_w_gate[l],
                              expert_w_up[l], expert_w_down[l])

        st_p.append((ka[:n_p].reshape(bp, tp, H_A, 2, DH_A), va[:n_p].reshape(bp, tp, H_A, VA),
                     ckv[:n_p].reshape(bp, tp, KV_LORA), kr[:n_p].reshape(bp, tp, ROPE_B), conv_p, s_p))
        st_s.append((ka[n_p:].reshape(bs, ts, H_A, 2, DH_A), va[n_p:].reshape(bs, ts, H_A, VA),
                     ckv[n_p:].reshape(bs, ts, KV_LORA), kr[n_p:].reshape(bs, ts, ROPE_B), conv_s, s_s))

    y = rmsnorm_rows(x, norm_final, F32)
    y_prompt = y[:n_p].reshape(bp, tp, D_MODEL)
    y_sample = y[n_p:].reshape(bs, ts, D_MODEL)

    def stk(sts, i):
        return jnp.stack([s[i] for s in sts])

    return (y_prompt, y_sample,
            stk(st_p, 0), stk(st_p, 1), stk(st_p, 2), stk(st_p, 3), stk(st_p, 4), stk(st_p, 5),
            stk(st_s, 0), stk(st_s, 1), stk(st_s, 2), stk(st_s, 3), stk(st_s, 4), stk(st_s, 5))
```

```python
import functools
import math

import jax
import jax.numpy as jnp
from jax import lax
from jax.experimental import pallas as pl
from jax.experimental.pallas import tpu as pltpu

F32 = jnp.float32
BF16 = jnp.bfloat16

D_MODEL = 2048
CHUNK = 64
CHUNK_SHIFT = 6
ROPE_THETA = 500000.0
EPS = 1e-6
N_BRANCH = 3
H_A = 8
DH_A = 64
VA = 2 * DH_A
ROT_A = DH_A // 4
H_B = 8
Q_LORA = 512
KV_LORA = 256
NOPE_B = 64
ROPE_B = 32
V_B = 128
H_C = 8
DK_C = 128
DV_C = 128
CONV_W = 4
QKV_C = 2 * H_C * DK_C + H_C * DV_C
BRANCH_W = H_A * VA
N_GROUPS = 4
EXPERTS_PER_GROUP = 4
N_EXPERTS = N_GROUPS * EXPERTS_PER_GROUP
D_EXPERT = 512

LANES = 128
VMEM_LIMIT_BYTES = 48 * 1024 * 1024
EXPERT_VMEM_LIMIT_BYTES = 56 * 1024 * 1024
NEG_BIG = -1e30

COL_AQ, COL_AK, COL_AV = 0, 1024, 2048
COL_DQ, COL_DKV, COL_MISC = 3072, 3584, 3840
COL_CQ, COL_CK, COL_CV, COL_CZ = 4096, 5120, 6144, 7168
SLAB_F = 8192
MISC_B0 = ROPE_B
MISC_A0 = ROPE_B + H_C

MOE_ROWS = 256
SLAB_ROWS = D_MODEL // LANES
ATTN_TQ = 512
ATTN_TK = 512


def _cparams(*sem):
    return pltpu.CompilerParams(dimension_semantics=sem, vmem_limit_bytes=VMEM_LIMIT_BYTES)


def _sigmoid(x):
    return 1.0 / (1.0 + jnp.exp(-x))


def _silu(x):
    return x * _sigmoid(x)


def _rmsnorm_kernel(x_ref, w_ref, o_ref):
    x = x_ref[...]
    ms = jnp.mean(x * x, axis=-1, keepdims=True)
    o_ref[...] = (x * lax.rsqrt(ms + EPS) * w_ref[...]).astype(o_ref.dtype)


def rmsnorm_rows(x, w, out_dtype, tm=512):
    m, d = x.shape
    return pl.pallas_call(
        _rmsnorm_kernel,
        grid=(m // tm,),
        in_specs=[pl.BlockSpec((tm, d), lambda i: (i, 0)),
                  pl.BlockSpec((1, d), lambda i: (0, 0))],
        out_specs=pl.BlockSpec((tm, d), lambda i: (i, 0)),
        out_shape=jax.ShapeDtypeStruct((m, d), out_dtype),
        compiler_params=_cparams("parallel"),
        name="rmsnorm",
    )(x, w.reshape(1, d))


def _rmsnorm_split_kernel(x_ref, w_ref, o1_ref, o2_ref, *, n1):
    x = x_ref[...]
    y = x * lax.rsqrt(jnp.mean(x * x, axis=-1, keepdims=True) + EPS) * w_ref[...]
    i = pl.program_id(0)

    @pl.when(i < n1)
    def _():
        o1_ref[...] = y

    @pl.when(i >= n1)
    def _():
        o2_ref[...] = y


def rmsnorm_split(x, w, rows1, tm=512):
    m, d = x.shape
    n1 = rows1 // tm
    return pl.pallas_call(
        functools.partial(_rmsnorm_split_kernel, n1=n1),
        grid=(m // tm,),
        in_specs=[pl.BlockSpec((tm, d), lambda i: (i, 0)),
                  pl.BlockSpec((1, d), lambda i: (0, 0))],
        out_specs=[pl.BlockSpec((tm, d), lambda i: (jnp.minimum(i, n1 - 1), 0)),
                   pl.BlockSpec((tm, d), lambda i: (jnp.maximum(i - n1, 0), 0))],
        out_shape=[jax.ShapeDtypeStruct((rows1, d), F32), jax.ShapeDtypeStruct((m - rows1, d), F32)],
        compiler_params=_cparams("arbitrary"),
        name="rmsnorm_final",
    )(x, w.reshape(1, d))


def _mm_kernel(a_ref, b_ref, o_ref, *, act):
    acc = jnp.dot(a_ref[...], b_ref[...], preferred_element_type=F32)
    if act == "sigmoid":
        acc = _sigmoid(acc)
    o_ref[...] = acc.astype(o_ref.dtype)


def _mm_res_kernel(a_ref, b_ref, r_ref, o_ref):
    acc = jnp.dot(a_ref[...], b_ref[...], preferred_element_type=F32)
    o_ref[...] = r_ref[...] + acc


def matmul(a, b, layer, out_dtype, act=None, residual=None, tm=512, tn=1024, name="mm"):
    m, k = a.shape
    n = b.shape[2]
    in_specs = [pl.BlockSpec((tm, k), lambda i, j: (i, 0)),
                pl.BlockSpec((None, k, tn), lambda i, j: (layer, 0, j))]
    args = [a, b]
    if residual is None:
        body = functools.partial(_mm_kernel, act=act)
    else:
        body = _mm_res_kernel
        in_specs.append(pl.BlockSpec((tm, tn), lambda i, j: (i, j)))
        args.append(residual)
    return pl.pallas_call(
        body,
        grid=(m // tm, n // tn),
        in_specs=in_specs,
        out_specs=pl.BlockSpec((tm, tn), lambda i, j: (i, j)),
        out_shape=jax.ShapeDtypeStruct((m, n), out_dtype),
        compiler_params=_cparams("parallel", "parallel"),
        name=name,
    )(*args)


def _rope_tables(pos, period, offset, half):
    rot = 2 * half
    inv = ROPE_THETA ** (-jnp.arange(half, dtype=F32) * 2.0 / rot)
    ang = pos.astype(F32)[:, None] * inv[None, :]
    cos, sin = jnp.cos(ang), jnp.sin(ang)
    lane = jnp.arange(LANES)
    r = (lane % period) - offset
    first = (r >= 0) & (r < half)
    second = (r >= half) & (r < rot)
    idx = jnp.clip(jnp.where(second, r - half, r), 0, half - 1)
    cos_l, sin_l = cos[:, idx], sin[:, idx]
    c = jnp.where((first | second)[None, :], cos_l, 1.0)
    sa = jnp.where(first[None, :], -sin_l, 0.0)
    sb = jnp.where(second[None, :], sin_l, 0.0)
    return c, sa, sb


def _rope_lanes(x, c, sa, sb, half):
    return (x * c + pltpu.roll(x, LANES - half, axis=1) * sa
            + pltpu.roll(x, half, axis=1) * sb)


def _prep_kernel(aq_ref, ak_ref, av_ref, dq_ref, dkv_ref, misc_ref,
                 ca_ref, saa_ref, sba_ref, cb_ref, sab_ref, sbb_ref, ck_ref, sak_ref, sbk_ref,
                 qn_ref, wuq_ref, kvn_ref,
                 qa_o, ka_o, kabf_o, va_o, vabf_o, qb_o, ckv_o, kr_o):
    ca, saa, sba = ca_ref[...], saa_ref[...], sba_ref[...]
    for c in range(H_A):
        sl = slice(c * LANES, (c + 1) * LANES)
        q = _rope_lanes(aq_ref[:, sl], ca, saa, sba, ROT_A // 2)
        qa_o[:, sl] = (q * (DH_A ** -0.5)).astype(BF16)
        k = _rope_lanes(ak_ref[:, sl], ca, saa, sba, ROT_A // 2)
        ka_o[:, sl] = k
        kabf_o[:, sl] = k.astype(BF16)
    v = av_ref[...]
    va_o[...] = v
    vabf_o[...] = v.astype(BF16)

    x = dq_ref[...]
    cq = x * lax.rsqrt(jnp.mean(x * x, axis=-1, keepdims=True) + EPS) * qn_ref[...]
    qb = jnp.dot(cq.astype(BF16), wuq_ref[...], preferred_element_type=F32)
    cb, sab, sbb = cb_ref[...], sab_ref[...], sbb_ref[...]
    scale_b = (NOPE_B + ROPE_B) ** -0.5
    for c in range(H_B):
        sl = slice(c * LANES, (c + 1) * LANES)
        q = _rope_lanes(qb[:, sl], cb, sab, sbb, ROPE_B // 2)
        qb_o[:, sl] = (q * scale_b).astype(BF16)

    x = dkv_ref[...]
    ckv_o[...] = x * lax.rsqrt(jnp.mean(x * x, axis=-1, keepdims=True) + EPS) * kvn_ref[...]
    kr = _rope_lanes(misc_ref[...], ck_ref[...], sak_ref[...], sbk_ref[...], ROPE_B // 2)
    kr_o[...] = kr[:, :ROPE_B]


def mixer_prep(slab, tabs, q_norm, w_uq_pad, kv_norm, tm=256):
    m = slab.shape[0]

    def col(width, start):
        blk = start // width
        return pl.BlockSpec((tm, width), lambda i: (i, blk))

    def row(width):
        return pl.BlockSpec((tm, width), lambda i: (i, 0))

    def full(shape):
        return pl.BlockSpec(shape, lambda i: (0,) * len(shape))

    in_specs = [col(1024, COL_AQ), col(1024, COL_AK), col(1024, COL_AV),
                col(Q_LORA, COL_DQ), col(KV_LORA, COL_DKV), col(LANES, COL_MISC)]
    in_specs += [row(LANES)] * 9
    in_specs += [full((1, Q_LORA)), full((Q_LORA, 1024)), full((1, KV_LORA))]
    out_shape = [jax.ShapeDtypeStruct((m, 1024), BF16),
                 jax.ShapeDtypeStruct((m, 1024), F32),
                 jax.ShapeDtypeStruct((m, 1024), BF16),
                 jax.ShapeDtypeStruct((m, 1024), F32),
                 jax.ShapeDtypeStruct((m, 1024), BF16),
                 jax.ShapeDtypeStruct((m, 1024), BF16),
                 jax.ShapeDtypeStruct((m, KV_LORA), F32),
                 jax.ShapeDtypeStruct((m, ROPE_B), F32)]
    out_specs = [row(1024)] * 6 + [row(KV_LORA), row(ROPE_B)]
    return pl.pallas_call(
        _prep_kernel,
        grid=(m // tm,),
        in_specs=in_specs,
        out_specs=out_specs,
        out_shape=out_shape,
        compiler_params=_cparams("parallel"),
        name="mixer_prep",
    )(slab, slab, slab, slab, slab, slab, *tabs,
      q_norm.reshape(1, Q_LORA), w_uq_pad, kv_norm.reshape(1, KV_LORA))


def _expand_kernel(ckv_ref, kr_ref, wuk_ref, pk_ref, wuv_ref, kb_o, vb_o):
    ckv = ckv_ref[...].astype(BF16)
    kr = kr_ref[...].astype(BF16)
    kb = (jnp.dot(ckv, wuk_ref[...], preferred_element_type=F32)
          + jnp.dot(kr, pk_ref[...], preferred_element_type=F32))
    kb_o[...] = kb.astype(BF16)
    vb_o[...] = jnp.dot(ckv, wuv_ref[...], preferred_element_type=F32).astype(BF16)


def mla_expand(ckv, kr, w_uk_pad, p_kr, w_uv, tm=512):
    m = ckv.shape[0]
    return pl.pallas_call(
        _expand_kernel,
        grid=(m // tm,),
        in_specs=[pl.BlockSpec((tm, KV_LORA), lambda i: (i, 0)),
                  pl.BlockSpec((tm, ROPE_B), lambda i: (i, 0)),
                  pl.BlockSpec((KV_LORA, 1024), lambda i: (0, 0)),
                  pl.BlockSpec((ROPE_B, 1024), lambda i: (0, 0)),
                  pl.BlockSpec((KV_LORA, 1024), lambda i: (0, 0))],
        out_specs=[pl.BlockSpec((tm, 1024), lambda i: (i, 0))] * 2,
        out_shape=[jax.ShapeDtypeStruct((m, 1024), BF16)] * 2,
        compiler_params=_cparams("parallel"),
        name="mla_expand",
    )(ckv, kr, w_uk_pad, p_kr, w_uv)


def _attn_kernel(*refs, diff, tq, tk, q_off, nk, lam_init, aliased, group):
    q_ref, k_ref, v_ref = refs[:3]
    pos = 3
    if diff:
        lamp_ref, subln_ref = refs[3:5]
        pos = 5
    if aliased:
        pos += 1
    single = nk == 1
    if single:
        o_ref, bias_sc = refs[pos:pos + 2]
    else:
        o_ref, bias_sc, m_sc, l_sc, acc_sc = refs[pos:pos + 5]
    n_maps = 2 if diff else 1
    maps = [(h, mp) for h in range(H_A) for mp in range(n_maps)]
    i = pl.program_id(1)
    j = pl.program_id(2)

    def write_head(h, outs):
        if diff:
            lp = lamp_ref[...]
            lam = (jnp.exp(jnp.sum(lp[0:1] * lp[1:2], axis=-1, keepdims=True))
                   - jnp.exp(jnp.sum(lp[2:3] * lp[3:4], axis=-1, keepdims=True)) + lam_init)
            o = outs[0] - lam * outs[1]
            o = o * lax.rsqrt(jnp.mean(o * o, axis=-1, keepdims=True) + EPS) * subln_ref[...]
            o = o * (1.0 - lam_init)
        else:
            o = outs[0]
        o_ref[:, h * LANES:(h + 1) * LANES] = o.astype(o_ref.dtype)

    if not single:
        @pl.when(j == 0)
        def _():
            m_sc[...] = jnp.full(m_sc.shape, NEG_BIG, F32)
            l_sc[...] = jnp.zeros(l_sc.shape, F32)
            acc_sc[...] = jnp.zeros(acc_sc.shape, F32)

    q_chunk_max = (q_off + i * tq + tq - 1) // CHUNK

    @pl.when((j * tk) // CHUNK <= q_chunk_max)
    def _():
        q_chunk = lax.shift_right_logical(q_off + i * tq + lax.broadcasted_iota(jnp.int32, (tq, 1), 0),
                                          CHUNK_SHIFT)
        k_chunk = lax.shift_right_logical(j * tk + lax.broadcasted_iota(jnp.int32, (1, tk), 1), CHUNK_SHIFT)
        bias_sc[...] = jnp.where(q_chunk >= k_chunk, 0.0, NEG_BIG)
        lane = lax.broadcasted_iota(jnp.int32, (1, LANES), 1)

        def scores(h, mp):
            hs = slice(h * LANES, (h + 1) * LANES)
            q = q_ref[:, hs]
            if diff:
                in_map = (lane < DH_A) if mp == 0 else (lane >= DH_A)
                q = jnp.where(in_map, q, jnp.zeros_like(q))
            return lax.dot_general(q, k_ref[:, hs], (((1,), (1,)), ((), ())),
                                   preferred_element_type=F32) + bias_sc[...]

        def pv(p, h):
            return jnp.dot(p.astype(BF16), v_ref[:, h * LANES:(h + 1) * LANES], preferred_element_type=F32)

        for g0 in range(0, len(maps), group):
            grp = maps[g0:g0 + group]
            s = [scores(h, mp) for h, mp in grp]
            m_cur = [jnp.max(x, axis=-1, keepdims=True) for x in s]
            if single:
                p = [jnp.exp(x - m) for x, m in zip(s, m_cur)]
                l = [jnp.sum(x, axis=-1, keepdims=True) for x in p]
                outs = [pv(x, h) / d for x, d, (h, _) in zip(p, l, grp)]
                for t in range(0, len(grp), n_maps):
                    write_head(grp[t][0], outs[t:t + n_maps])
            else:
                ids = [h * n_maps + mp for h, mp in grp]
                m_prev = [m_sc[t] for t in ids]
                m_new = [jnp.maximum(a, b) for a, b in zip(m_prev, m_cur)]
                p = [jnp.exp(x - jnp.concatenate([m] * (tk // LANES), axis=1)) for x, m in zip(s, m_new)]
                alpha = [jnp.exp(a - b) for a, b in zip(m_prev, m_new)]
                for t, a, x, m, (h, _) in zip(ids, alpha, p, m_new, grp):
                    l_sc[t] = a * l_sc[t] + jnp.sum(x, axis=-1, keepdims=True)
                    acc_sc[t] = a * acc_sc[t] + pv(x, h)
                    m_sc[t] = m

    if not single:
        @pl.when(j == nk - 1)
        def _():
            for h in range(H_A):
                write_head(h, [acc_sc[h * n_maps + mp] / l_sc[h * n_maps + mp] for mp in range(n_maps)])


def attention(q, k, v, out_prev, *, n_streams, q_row0, t_q, t_k, q_off, tq, tk, diff,
              lamp=None, subln=None, lam_init=0.0, group=2, name="attn"):
    nq, nk = t_q // tq, t_k // tk
    qb0 = q_row0 // tq
    width = H_A * LANES

    def q_map(b, i, j):
        return (qb0 + b * nq + i, 0)

    def kv_map(b, i, j):
        last = ((q_off + i * tq + tq - 1) // CHUNK * CHUNK) // tk
        return (b * nk + jnp.minimum(j, last), 0)

    in_specs = [pl.BlockSpec((tq, width), q_map),
                pl.BlockSpec((tk, width), kv_map),
                pl.BlockSpec((tk, width), kv_map)]
    args = [q, k, v]
    n_maps = 1
    if diff:
        n_maps = 2
        in_specs += [pl.BlockSpec((4, DH_A), lambda b, i, j: (0, 0)),
                     pl.BlockSpec((1, VA), lambda b, i, j: (0, 0))]
        args += [lamp, subln.reshape(1, VA)]
    aliases = {}
    if out_prev is not None:
        aliases = {len(args): 0}
        in_specs.append(pl.BlockSpec(memory_space=pl.ANY))
        args.append(out_prev)
    return pl.pallas_call(
        functools.partial(_attn_kernel, diff=diff, tq=tq, tk=tk, q_off=q_off, nk=nk, lam_init=lam_init,
                          aliased=out_prev is not None, group=group),
        grid=(n_streams, nq, nk),
        in_specs=in_specs,
        out_specs=pl.BlockSpec((tq, width), q_map),
        out_shape=jax.ShapeDtypeStruct((q.shape[0], width), BF16),
        scratch_shapes=[pltpu.VMEM((tq, tk), F32)] + ([] if nk == 1 else
                                                      [pltpu.VMEM((H_A * n_maps, tq, LANES), F32)] * 3),
        input_output_aliases=aliases,
        compiler_params=_cparams("parallel", "parallel", "arbitrary"),
        name=name,
    )(*args)


def _attn_cached_kernel(q_ref, kn_ref, vn_ref, kc_ref, vc_ref, lamp_ref, subln_ref, prev_ref, o_ref, bias_sc,
                        *, ts, past, lam_init):
    del prev_ref
    pos = past + lax.broadcasted_iota(jnp.int32, (ts, 1), 0)
    kpos = past + lax.broadcasted_iota(jnp.int32, (1, ts), 1)
    bias_sc[...] = jnp.where(lax.shift_right_logical(pos, CHUNK_SHIFT)
                             >= lax.shift_right_logical(kpos, CHUNK_SHIFT), 0.0, NEG_BIG)
    lane = lax.broadcasted_iota(jnp.int32, (1, LANES), 1)
    lp = lamp_ref[...]
    lam = (jnp.exp(jnp.sum(lp[0:1] * lp[1:2], axis=-1, keepdims=True))
           - jnp.exp(jnp.sum(lp[2:3] * lp[3:4], axis=-1, keepdims=True)) + lam_init)
    nt = (((1,), (1,)), ((), ()))
    for h in range(H_A):
        hs = slice(h * LANES, (h + 1) * LANES)
        q = q_ref[:, hs]
        qs = [jnp.where(lane < DH_A, q, jnp.zeros_like(q)), jnp.where(lane >= DH_A, q, jnp.zeros_like(q))]
        kc = kc_ref[0, :, hs].astype(BF16)
        vc = vc_ref[0, :, hs].astype(BF16)
        kn = kn_ref[:, hs]
        vn = vn_ref[:, hs]
        s_old = [lax.dot_general(x, kc, nt, preferred_element_type=F32) for x in qs]
        s_new = [lax.dot_general(x, kn, nt, preferred_element_type=F32) + bias_sc[...] for x in qs]
        m = [jnp.maximum(jnp.max(a, axis=-1, keepdims=True), jnp.max(b, axis=-1, keepdims=True))
             for a, b in zip(s_old, s_new)]
        p_old = [jnp.exp(a - t) for a, t in zip(s_old, m)]
        p_new = [jnp.exp(b - t) for b, t in zip(s_new, m)]
        l = [jnp.sum(a, axis=-1, keepdims=True) + jnp.sum(b, axis=-1, keepdims=True)
             for a, b in zip(p_old, p_new)]
        outs = [(jnp.dot(a.astype(BF16), vc, preferred_element_type=F32)
                 + jnp.dot(b.astype(BF16), vn, preferred_element_type=F32)) / d
                for a, b, d in zip(p_old, p_new, l)]
        o = outs[0] - lam * outs[1]
        o = o * lax.rsqrt(jnp.mean(o * o, axis=-1, keepdims=True) + EPS) * subln_ref[...]
        o_ref[:, hs] = (o * (1.0 - lam_init)).astype(o_ref.dtype)


def attention_cached(q, k_new, v_new, cache_k, cache_v, out_prev, *, row0, lamp, subln, lam_init):
    bs, past, width = cache_k.shape
    ts = (q.shape[0] - row0) // bs
    blk0 = row0 // ts
    new_spec = pl.BlockSpec((ts, width), lambda b: (blk0 + b, 0))
    cache_spec = pl.BlockSpec((1, past, width), lambda b: (b, 0, 0))
    return pl.pallas_call(
        functools.partial(_attn_cached_kernel, ts=ts, past=past, lam_init=lam_init),
        grid=(bs,),
        in_specs=[new_spec, new_spec, new_spec, cache_spec, cache_spec,
                  pl.BlockSpec((4, DH_A), lambda b: (0, 0)),
                  pl.BlockSpec((1, VA), lambda b: (0, 0)),
                  pl.BlockSpec(memory_space=pl.ANY)],
        out_specs=new_spec,
        out_shape=jax.ShapeDtypeStruct(out_prev.shape, BF16),
        scratch_shapes=[pltpu.VMEM((ts, ts), F32)],
        input_output_aliases={7: 0},
        compiler_params=_cparams("parallel"),
        name="diff_attn_s",
    )(q, k_new, v_new, cache_k, cache_v, lamp, subln.reshape(1, VA), out_prev)


def _mm_nt(a, b):
    return lax.dot_general(a, b, (((1,), (1,)), ((), ())), preferred_element_type=F32)


def _mm_nn(a, b):
    return jnp.dot(a, b, preferred_element_type=F32)


def _mm_tn(a, b):
    return lax.dot_general(a, b, (((0,), (0,)), ((), ())), preferred_element_type=F32)


def _bf(xs):
    return [x.astype(BF16) for x in xs]


def _shift_rows(x, tail, s):
    xs = pltpu.roll(x, s, axis=0)
    ts = pltpu.roll(tail, s, axis=0)
    row = lax.broadcasted_iota(jnp.int32, (8, 1), 0)
    top = jnp.where(row < s, ts, xs[0:8])
    return jnp.concatenate([top, xs[8:]], axis=0)


def _gdn_kernel(*refs, n_chunks, aliased):
    (cq_ref, ck_ref, cv_ref, cz_ref, misc_ref, convp_ref, s0_ref,
     kern_ref, alog_ref, dtb_ref, norm_ref) = refs[:11]
    o_ref, s_o, s_sc, tail_sc = refs[(12 if aliased else 11):]
    n = pl.program_id(1)
    L = CHUNK
    heads = range(H_C)

    @pl.when(n == 0)
    def _():
        s_sc[...] = s0_ref[0]
        tail_sc[...] = jnp.zeros(tail_sc.shape, F32)
        tail_sc[8 - (CONV_W - 1):8, :] = convp_ref[0]

    conv = []
    for part, ref in enumerate((cq_ref, ck_ref, cv_ref)):
        cs = slice(part * 1024, (part + 1) * 1024)
        x = ref[...]
        tail = tail_sc[:, cs]
        acc = x * kern_ref[CONV_W - 1:CONV_W, cs]
        for s in range(1, CONV_W):
            acc = acc + _shift_rows(x, tail, s) * kern_ref[CONV_W - 1 - s:CONV_W - s, cs]
        conv.append(_silu(acc))
        tail_sc[:, cs] = x[L - 8:L]
    qc, kc, vc = conv

    misc = misc_ref[...]
    beta_all = _sigmoid(misc)
    za = misc + dtb_ref[...]
    softplus = jnp.maximum(za, 0.0) + jnp.log(1.0 + jnp.exp(-jnp.abs(za)))
    g_all = -jnp.exp(alog_ref[...]) * softplus
    row = lax.broadcasted_iota(jnp.int32, (L, 1), 0)
    gcum_all = g_all
    for s in (1, 2, 4, 8, 16, 32):
        gcum_all = gcum_all + jnp.where(row >= s, pltpu.roll(gcum_all, s, axis=0), 0.0)
    lane = lax.broadcasted_iota(jnp.int32, (1, LANES), 1)
    is_beta = (lane >= MISC_B0) & (lane < MISC_B0 + H_C)
    packed = jnp.where(is_beta, beta_all, gcum_all)
    packed_t = jnp.concatenate([packed, jnp.zeros((LANES - L, LANES), F32)], axis=0).T

    ri = lax.broadcasted_iota(jnp.int32, (L, L), 0)
    ci = lax.broadcasted_iota(jnp.int32, (L, L), 1)
    incl = ri >= ci
    strict = ri > ci
    diag16 = (ri // 16) == (ci // 16)

    qn, kn, v_h, beta, gcum, decay, gam = [], [], [], [], [], [], []
    for h in heads:
        hs = slice(h * LANES, (h + 1) * LANES)
        qh, kh = qc[:, hs], kc[:, hs]
        qn.append(qh * lax.rsqrt(jnp.sum(qh * qh, axis=-1, keepdims=True) + EPS) * (DK_C ** -0.5))
        kn.append(kh * lax.rsqrt(jnp.sum(kh * kh, axis=-1, keepdims=True) + EPS))
        v_h.append(vc[:, hs])
        beta.append(beta_all[:, MISC_B0 + h:MISC_B0 + h + 1])
        gc = gcum_all[:, MISC_A0 + h:MISC_A0 + h + 1]
        gcum.append(gc)
        seg = gc - packed_t[MISC_A0 + h:MISC_A0 + h + 1, 0:L]
        decay.append(jnp.where(incl, jnp.exp(jnp.where(incl, seg, 0.0)), 0.0))
        gam.append(jnp.exp(gc))
    q16, k16 = _bf(qn), _bf(kn)
    kk = [_mm_nt(k16[h], k16[h]) for h in heads]
    pq = [_mm_nt(q16[h], k16[h]) * decay[h] for h in heads]
    a_mat = [jnp.where(strict, beta[h] * kk[h] * decay[h], 0.0) for h in heads]

    a_d = [jnp.where(diag16, a_mat[h], 0.0) for h in heads]
    a_lo = [a_mat[h] - a_d[h] for h in heads]
    ad16 = _bf(a_d)
    a2 = [_mm_nn(ad16[h], ad16[h]) for h in heads]
    a2_16 = _bf(a2)
    a4 = [_mm_nn(a2_16[h], a2_16[h]) for h in heads]
    a4_16 = _bf(a4)
    a8 = [_mm_nn(a4_16[h], a4_16[h]) for h in heads]
    f12 = [a2[h] - a_d[h] - _mm_nn(ad16[h], a2_16[h]) for h in heads]
    a8_16 = _bf(a8)
    f34 = [a4[h] + a8[h] + _mm_nn(a4_16[h], a8_16[h]) for h in heads]
    f12_16, f34_16 = _bf(f12), _bf(f34)
    g = [f12[h] + f34[h] + _mm_nn(f12_16[h], f34_16[h]) for h in heads]
    g16, alo16 = _bf(g), _bf(a_lo)
    nm = [a_lo[h] + _mm_nn(g16[h], alo16[h]) for h in heads]
    n16 = _bf(nm)
    n2 = [_mm_nn(n16[h], n16[h]) for h in heads]
    n2_16 = _bf(n2)
    hm = [n2[h] - nm[h] - _mm_nn(n16[h], n2_16[h]) for h in heads]
    hm16 = _bf(hm)
    tm1 = [hm[h] + g[h] + _mm_nn(hm16[h], g16[h]) for h in heads]
    tm1_16 = _bf(tm1)
    rhs = [jnp.concatenate([v_h[h] * beta[h], kn[h] * (beta[h] * gam[h])], axis=-1) for h in heads]
    rhs16 = _bf(rhs)
    sol = [rhs[h] + _mm_nn(tm1_16[h], rhs16[h]) for h in heads]

    s_prev = [s_sc[h] for h in heads]
    s16 = _bf(s_prev)
    w16 = _bf([sol[h][:, DV_C:] for h in heads])
    u = [sol[h][:, :DV_C] - _mm_nt(w16[h], s16[h]) for h in heads]
    u16 = _bf(u)
    pq16 = _bf(pq)
    o = [gam[h] * _mm_nt(q16[h], s16[h]) + _mm_nn(pq16[h], u16[h]) for h in heads]
    for h in heads:
        gcum_end = gcum[h][L - 1:L, :]
        ut = (u[h] * jnp.exp(gcum_end - gcum[h])).astype(BF16)
        s_sc[h] = jnp.exp(gcum_end) * s_prev[h] + _mm_tn(ut, k16[h])
    for h in heads:
        hs = slice(h * LANES, (h + 1) * LANES)
        oh = o[h] * lax.rsqrt(jnp.mean(o[h] * o[h], axis=-1, keepdims=True) + EPS) * norm_ref[...]
        o_ref[:, hs] = (oh * _silu(cz_ref[:, hs])).astype(o_ref.dtype)

    @pl.when(n == n_chunks - 1)
    def _():
        s_o[0] = s_sc[...]


def gated_delta(slab, conv_prev, s0, kern, alog_row, dtb_row, norm_w, out_prev, *, row_blk0, n_streams, n_chunks):
    def rows(b, n):
        return row_blk0 + b * n_chunks + n

    def col(start):
        blk = start // 1024
        return pl.BlockSpec((CHUNK, 1024), lambda b, n: (rows(b, n), blk))

    def full(shape):
        return pl.BlockSpec(shape, lambda b, n: (0,) * len(shape))

    in_specs = [col(COL_CQ), col(COL_CK), col(COL_CV), col(COL_CZ),
                pl.BlockSpec((CHUNK, LANES), lambda b, n: (rows(b, n), COL_MISC // LANES)),
                pl.BlockSpec((1, CONV_W - 1, QKV_C), lambda b, n: (b, 0, 0)),
                pl.BlockSpec((1, H_C, DV_C, DK_C), lambda b, n: (b, 0, 0, 0)),
                full((CONV_W, QKV_C)), full((1, LANES)), full((1, LANES)), full((1, DV_C))]
    args = [slab, slab, slab, slab, slab, conv_prev, s0, kern, alog_row, dtb_row, norm_w.reshape(1, DV_C)]
    aliases = {}
    if out_prev is not None:
        aliases = {len(args): 0}
        in_specs.append(pl.BlockSpec(memory_space=pl.ANY))
        args.append(out_prev)
    return pl.pallas_call(
        functools.partial(_gdn_kernel, n_chunks=n_chunks, aliased=out_prev is not None),
        grid=(n_streams, n_chunks),
        in_specs=in_specs,
        out_specs=[pl.BlockSpec((CHUNK, 1024), lambda b, n: (rows(b, n), 0)),
                   pl.BlockSpec((1, H_C, DV_C, DK_C), lambda b, n: (b, 0, 0, 0))],
        out_shape=[jax.ShapeDtypeStruct((slab.shape[0], 1024), BF16),
                   jax.ShapeDtypeStruct((n_streams, H_C, DV_C, DK_C), F32)],
        scratch_shapes=[pltpu.VMEM((H_C, DV_C, DK_C), F32),
                        pltpu.VMEM((8, QKV_C), F32)],
        input_output_aliases=aliases,
        compiler_params=_cparams("parallel", "arbitrary"),
        name="gated_delta",
    )(*args)


def _merge_kernel(oa_ref, ob_ref, oc_ref, wb_ref, ga_ref, gb_ref, gc_ref, o_ref):
    acc = jnp.dot(oa_ref[...], wb_ref[0], preferred_element_type=F32) * ga_ref[...].astype(F32)
    acc = acc + jnp.dot(ob_ref[...], wb_ref[1], preferred_element_type=F32) * gb_ref[...].astype(F32)
    acc = acc + jnp.dot(oc_ref[...], wb_ref[2], preferred_element_type=F32) * gc_ref[...].astype(F32)
    o_ref[...] = acc.astype(o_ref.dtype)


def branch_merge(oa, ob, oc, w_branch, layer, gates, tm=512, tn=1024):
    m = oa.shape[0]
    nj = D_MODEL // tn

    def gate_spec(nb):
        return pl.BlockSpec((tm, tn), lambda i, j: (i, nb * nj + j))

    return pl.pallas_call(
        _merge_kernel,
        grid=(m // tm, nj),
        in_specs=[pl.BlockSpec((tm, BRANCH_W), lambda i, j: (i, 0))] * 3
                 + [pl.BlockSpec((None, N_BRANCH, BRANCH_W, tn), lambda i, j: (layer, 0, 0, j))]
                 + [gate_spec(0), gate_spec(1), gate_spec(2)],
        out_specs=pl.BlockSpec((tm, tn), lambda i, j: (i, j)),
        out_shape=jax.ShapeDtypeStruct((m, D_MODEL), BF16),
        compiler_params=_cparams("parallel", "parallel"),
        name="branch_merge",
    )(oa, ob, oc, w_branch, gates, gates, gates)


def _slab_store(ref, x, rows):
    for c in range(SLAB_ROWS):
        ref[pl.ds(c, rows, stride=SLAB_ROWS), :] = x[:, c * LANES:(c + 1) * LANES]


def _slab_load(ref, rows):
    return jnp.concatenate([ref[pl.ds(c, rows, stride=SLAB_ROWS), :] for c in range(SLAB_ROWS)], axis=1)


def _router_kernel(x_ref, nw_ref, wr_ref, br_ref, h_o, info_o, cnt_o, cnt_sc, tri_sc, *, tm):
    step = pl.program_id(0)

    @pl.when(step == 0)
    def _():
        cnt_sc[...] = jnp.zeros(cnt_sc.shape, F32)
        ri = lax.broadcasted_iota(jnp.int32, (tm, tm), 0)
        ci = lax.broadcasted_iota(jnp.int32, (tm, tm), 1)
        tri_sc[...] = (ri > ci).astype(BF16)

    x = x_ref[...]
    h = (x * lax.rsqrt(jnp.mean(x * x, axis=-1, keepdims=True) + EPS) * nw_ref[...])
    _slab_store(h_o, h, tm)
    logits = jnp.dot(h.astype(BF16), wr_ref[...], preferred_element_type=F32) + br_ref[...]
    lane = lax.broadcasted_iota(jnp.int32, logits.shape, 1)
    is_g = lane < N_GROUPS
    lg = jnp.where(is_g, logits, NEG_BIG)
    eg = jnp.where(is_g, jnp.exp(lg - jnp.max(lg, axis=-1, keepdims=True)), 0.0)
    gp = eg / jnp.sum(eg, axis=-1, keepdims=True)
    pg = jnp.max(gp, axis=-1, keepdims=True)
    grp = jnp.min(jnp.where(is_g & (gp == pg), lane, LANES), axis=-1, keepdims=True)
    lo = N_GROUPS + grp * EXPERTS_PER_GROUP
    is_e = (lane >= lo) & (lane < lo + EXPERTS_PER_GROUP)
    le = jnp.where(is_e, logits, NEG_BIG)
    ee = jnp.where(is_e, jnp.exp(le - jnp.max(le, axis=-1, keepdims=True)), 0.0)
    ep = ee / jnp.sum(ee, axis=-1, keepdims=True)
    v1 = jnp.max(jnp.where(is_e, ep, -1.0), axis=-1, keepdims=True)
    i1 = jnp.min(jnp.where(is_e & (ep == v1), lane, LANES), axis=-1, keepdims=True)
    rest = is_e & (lane != i1)
    v2 = jnp.max(jnp.where(rest, ep, -1.0), axis=-1, keepdims=True)
    i2 = jnp.min(jnp.where(rest & (ep == v2), lane, LANES), axis=-1, keepdims=True)
    den = v1 + v2
    w1 = pg * v1 / den
    w2 = pg * v2 / den

    oh1 = lane == i1
    oh2 = lane == i2
    picked = (oh1 | oh2).astype(F32)
    before = jnp.dot(tri_sc[...], picked.astype(BF16), preferred_element_type=F32) + cnt_sc[...]
    rank1 = jnp.sum(jnp.where(oh1, before, 0.0), axis=-1, keepdims=True)
    rank2 = jnp.sum(jnp.where(oh2, before, 0.0), axis=-1, keepdims=True)
    cnt = cnt_sc[...] + jnp.sum(picked, axis=0, keepdims=True)
    cnt_sc[...] = cnt
    cnt_o[...] = cnt

    info = jnp.where(lane == 0, (i1 - N_GROUPS).astype(F32), 0.0)
    info = jnp.where(lane == 1, (i2 - N_GROUPS).astype(F32), info)
    info = jnp.where(lane == 2, w1, info)
    info = jnp.where(lane == 3, w2, info)
    info = jnp.where(lane == 4, rank1, info)
    info = jnp.where(lane == 5, rank2, info)
    info_o[...] = info


def moe_router(x, norm_w, w_router, b_router, tm=512):
    m = x.shape[0]
    return pl.pallas_call(
        functools.partial(_router_kernel, tm=tm),
        grid=(m // tm,),
        in_specs=[pl.BlockSpec((tm, D_MODEL), lambda i: (i, 0)),
                  pl.BlockSpec((1, D_MODEL), lambda i: (0, 0)),
                  pl.BlockSpec((D_MODEL, LANES), lambda i: (0, 0)),
                  pl.BlockSpec((1, LANES), lambda i: (0, 0))],
        out_specs=[pl.BlockSpec((tm * SLAB_ROWS, LANES), lambda i: (i, 0)),
                   pl.BlockSpec((tm, LANES), lambda i: (i, 0)),
                   pl.BlockSpec((1, LANES), lambda i: (0, 0))],
        out_shape=[jax.ShapeDtypeStruct((m * SLAB_ROWS, LANES), F32),
                   jax.ShapeDtypeStruct((m, LANES), F32),
                   jax.ShapeDtypeStruct((1, LANES), F32)],
        scratch_shapes=[pltpu.VMEM((1, LANES), F32), pltpu.VMEM((tm, tm), BF16)],
        compiler_params=_cparams("arbitrary"),
        name="moe_router",
    )(x, norm_w.reshape(1, D_MODEL), w_router, b_router)


def _slab_copy(src_ref, dst_ref, sem, tok, r):
    return pltpu.make_async_copy(src_ref.at[pl.ds(tok * SLAB_ROWS, SLAB_ROWS), :],
                                 dst_ref.at[pl.ds(r * SLAB_ROWS, SLAB_ROWS), :], sem)


def _dispatch_kernel(idx_ref, src_ref, o_ref, sem, *, rows):
    def start(r, c):
        _slab_copy(src_ref, o_ref, sem.at[0], idx_ref[0, 0, r], r).start()
        return c

    lax.fori_loop(0, rows, start, 0, unroll=8)

    def wait(r, c):
        _slab_copy(src_ref, o_ref, sem.at[0], idx_ref[0, 0, r], r).wait()
        return c

    lax.fori_loop(0, rows, wait, 0, unroll=8)


def dispatch_slabs(src, row_token, rows=MOE_ROWS):
    n_out = row_token.shape[0]
    nb = n_out // rows
    return pl.pallas_call(
        functools.partial(_dispatch_kernel, rows=rows),
        grid=(nb,),
        in_specs=[pl.BlockSpec((1, 1, rows), lambda i: (i, 0, 0), memory_space=pltpu.SMEM),
                  pl.BlockSpec(memory_space=pl.ANY)],
        out_specs=pl.BlockSpec((rows * SLAB_ROWS, LANES), lambda i: (i, 0)),
        out_shape=jax.ShapeDtypeStruct((n_out * SLAB_ROWS, LANES), F32),
        scratch_shapes=[pltpu.SemaphoreType.DMA((1,))],
        compiler_params=_cparams("arbitrary"),
        name="moe_dispatch",
    )(row_token.reshape(nb, 1, rows), src)


def _combine_kernel(i1_ref, i2_ref, src_ref, info_ref, res_ref, o_ref, buf1, buf2, sem, *, rows):
    def start(r, c):
        _slab_copy(src_ref, buf1, sem.at[0], i1_ref[0, 0, r], r).start()
        _slab_copy(src_ref, buf2, sem.at[1], i2_ref[0, 0, r], r).start()
        return c

    lax.fori_loop(0, rows, start, 0, unroll=8)

    def wait(r, c):
        _slab_copy(src_ref, buf1, sem.at[0], i1_ref[0, 0, r], r).wait()
        _slab_copy(src_ref, buf2, sem.at[1], i2_ref[0, 0, r], r).wait()
        return c

    lax.fori_loop(0, rows, wait, 0, unroll=8)
    info = info_ref[...]
    o_ref[...] = (res_ref[...] + info[:, 2:3] * _slab_load(buf1, rows)
                  + info[:, 3:4] * _slab_load(buf2, rows))


def combine_slabs(ys, dest1, dest2, info, residual, rows=MOE_ROWS):
    n = dest1.shape[0]
    nb = n // rows
    idx_spec = pl.BlockSpec((1, 1, rows), lambda i: (i, 0, 0), memory_space=pltpu.SMEM)
    return pl.pallas_call(
        functools.partial(_combine_kernel, rows=rows),
        grid=(nb,),
        in_specs=[idx_spec, idx_spec, pl.BlockSpec(memory_space=pl.ANY),
                  pl.BlockSpec((rows, LANES), lambda i: (i, 0)),
                  pl.BlockSpec((rows, D_MODEL), lambda i: (i, 0))],
        out_specs=pl.BlockSpec((rows, D_MODEL), lambda i: (i, 0)),
        out_shape=jax.ShapeDtypeStruct((n, D_MODEL), F32),
        scratch_shapes=[pltpu.VMEM((rows * SLAB_ROWS, LANES), F32)] * 2 + [pltpu.SemaphoreType.DMA((2,))],
        compiler_params=_cparams("arbitrary"),
        name="moe_combine",
    )(dest1.reshape(nb, 1, rows), dest2.reshape(nb, 1, rows), ys, info, residual)


def _expert_kernel(be_ref, nu_ref, x_ref, wg_ref, wu_ref, wd_ref, o_ref, wg_sc, wu_sc, wd_sc, *, rows):
    b = pl.program_id(0)
    prev = be_ref[jnp.maximum(b - 1, 0)]

    @pl.when((b == 0) | (be_ref[b] != prev))
    def _():
        wg_sc[...] = wg_ref[...].astype(BF16)
        wu_sc[...] = wu_ref[...].astype(BF16)
        wd_sc[...] = wd_ref[...].astype(BF16)

    @pl.when(b < nu_ref[0])
    def _():
        x = _slab_load(x_ref, rows).astype(BF16)
        g = jnp.dot(x, wg_sc[...], preferred_element_type=F32)
        u = jnp.dot(x, wu_sc[...], preferred_element_type=F32)
        hid = _silu(g) * u
        _slab_store(o_ref, jnp.dot(hid.astype(BF16), wd_sc[...], preferred_element_type=F32), rows)

    @pl.when(b >= nu_ref[0])
    def _():
        o_ref[...] = jnp.zeros(o_ref.shape, F32)


def routed_experts(xs, block_expert, n_used, w_gate, w_up, w_down, layer, rows=MOE_ROWS):
    nb = xs.shape[0] // (rows * SLAB_ROWS)
    grid_spec = pltpu.PrefetchScalarGridSpec(
        num_scalar_prefetch=2,
        grid=(nb,),
        in_specs=[pl.BlockSpec((rows * SLAB_ROWS, LANES), lambda b, be, nu: (b, 0)),
                  pl.BlockSpec((None, None, D_MODEL, D_EXPERT), lambda b, be, nu: (layer, be[b], 0, 0)),
                  pl.BlockSpec((None, None, D_MODEL, D_EXPERT), lambda b, be, nu: (layer, be[b], 0, 0)),
                  pl.BlockSpec((None, None, D_EXPERT, D_MODEL), lambda b, be, nu: (layer, be[b], 0, 0))],
        out_specs=pl.BlockSpec((rows * SLAB_ROWS, LANES), lambda b, be, nu: (b, 0)),
        scratch_shapes=[pltpu.VMEM((D_MODEL, D_EXPERT), BF16),
                        pltpu.VMEM((D_MODEL, D_EXPERT), BF16),
                        pltpu.VMEM((D_EXPERT, D_MODEL), BF16)],
    )
    return pl.pallas_call(
        functools.partial(_expert_kernel, rows=rows),
        grid_spec=grid_spec,
        out_shape=jax.ShapeDtypeStruct(xs.shape, F32),
        compiler_params=pltpu.CompilerParams(dimension_semantics=("arbitrary",),
                                             vmem_limit_bytes=EXPERT_VMEM_LIMIT_BYTES),
        name="routed_experts",
    )(block_expert, n_used, xs, w_gate, w_up, w_down)


def _routing_plan(e1, e2, rank1, rank2, count, rows):
    n = e1.shape[0]
    n_slots = 2 * n + N_EXPERTS * rows
    padded = (count + rows - 1) // rows * rows
    ends = jnp.cumsum(padded)
    starts = ends - padded
    dest1 = starts[e1] + rank1
    dest2 = starts[e2] + rank2
    tok = jnp.arange(n, dtype=jnp.int32)
    row_token = jnp.zeros((n_slots,), jnp.int32).at[jnp.concatenate([dest1, dest2])].set(
        jnp.concatenate([tok, tok]), unique_indices=True)
    blk_start = jnp.arange(n_slots // rows, dtype=jnp.int32) * rows
    block_expert = jnp.minimum(jnp.sum(ends[None, :] <= blk_start[:, None], axis=1), N_EXPERTS - 1).astype(jnp.int32)
    n_used = (ends[-1] // rows).astype(jnp.int32).reshape(1)
    return row_token, block_expert, n_used, dest1, dest2


def hier_moe_residual(x, norm_w, w_router, b_router, w_gate, w_up, w_down, layer):
    h, info, cnt = moe_router(x, norm_w, w_router, b_router)
    ints = info[:, 0:6].astype(jnp.int32)
    count = cnt[0, N_GROUPS:N_GROUPS + N_EXPERTS].astype(jnp.int32)
    row_token, block_expert, n_used, dest1, dest2 = _routing_plan(
        ints[:, 0], ints[:, 1], ints[:, 4], ints[:, 5], count, MOE_ROWS)
    xs = dispatch_slabs(h, row_token)
    ys = routed_experts(xs, block_expert, n_used, w_gate, w_up, w_down, layer)
    return combine_slabs(ys, dest1, dest2, info, x)


def _pack_w_in(w):
    sizes = (1024, 1024, 1024, Q_LORA, KV_LORA, ROPE_B, QKV_C, 1024, H_C, H_C, N_BRANCH * D_MODEL)
    offs = [0]
    for s in sizes:
        offs.append(offs[-1] + s)
    a_q, a_k, a_v, b_dq, b_dkv, b_kr, c_qkv, c_z, c_b, c_a, g_in = (
        w[:, :, offs[i]:offs[i + 1]] for i in range(len(sizes)))
    zeros = jnp.zeros((w.shape[0], D_MODEL, 2 * LANES - ROPE_B - 2 * H_C), w.dtype)
    slab = jnp.concatenate([a_q, a_k, a_v, b_dq, b_dkv, b_kr, c_b, c_a, zeros, c_qkv, c_z], axis=2)
    return slab.astype(BF16), g_in.astype(BF16)


def _pack_mla(w_uq, w_ukv):
    wq = w_uq.reshape(Q_LORA, H_B, NOPE_B + ROPE_B)
    wq = jnp.pad(wq, ((0, 0), (0, 0), (0, LANES - NOPE_B - ROPE_B))).reshape(Q_LORA, H_B * LANES)
    wkv = w_ukv.reshape(KV_LORA, H_B, NOPE_B + V_B)
    wk = jnp.pad(wkv[:, :, :NOPE_B], ((0, 0), (0, 0), (0, LANES - NOPE_B))).reshape(KV_LORA, H_B * LANES)
    wv = wkv[:, :, NOPE_B:].reshape(KV_LORA, H_B * V_B)
    place = jnp.zeros((ROPE_B, H_B, LANES), F32)
    place = place.at[jnp.arange(ROPE_B), :, NOPE_B + jnp.arange(ROPE_B)].set(1.0)
    return wq.astype(BF16), wk.astype(BF16), place.reshape(ROPE_B, H_B * LANES).astype(BF16), wv.astype(BF16)


def _misc_row(vals, lane0):
    return jnp.zeros((1, LANES), F32).at[0, lane0:lane0 + H_C].set(vals.astype(F32))


def kernel(x_prompt, x_sample, cache_diff_k, cache_diff_v, cache_mla_ckv, cache_mla_krope, state_gdn_conv, state_gdn_s, norm_mix, w_in, diff_lambda, diff_subln, mla_q_norm, mla_w_uq, mla_kv_norm, mla_w_ukv, gdn_conv, gdn_a_log, gdn_dt_bias, gdn_norm, w_branch, w_out, norm_ffn, router_group, router_group_bias, router_expert, router_expert_bias, expert_w_gate, expert_w_up, expert_w_down, norm_final):
    bp, tp, _ = x_prompt.shape
    bs, ts, _ = x_sample.shape
    depth = w_in.shape[0]
    past = cache_diff_k.shape[2]
    n_p, n_s = bp * tp, bs * ts
    tk_s = past + ts

    x = jnp.concatenate([x_prompt.reshape(n_p, D_MODEL), x_sample.reshape(n_s, D_MODEL)], axis=0)
    pos = jnp.concatenate([jnp.tile(jnp.arange(tp), bp), jnp.tile(past + jnp.arange(ts), bs)])
    tabs = (_rope_tables(pos, DH_A, 0, ROT_A // 2)
            + _rope_tables(pos, LANES, NOPE_B, ROPE_B // 2)
            + _rope_tables(pos, LANES, 0, ROPE_B // 2))

    zero_conv = jnp.zeros((bp, CONV_W - 1, QKV_C), F32)
    zero_state = jnp.zeros((bp, H_C, DV_C, DK_C), F32)
    zero_branch = jnp.zeros((n_p + n_s, BRANCH_W), BF16)
    st_p, st_s = [], []
    w_slab, w_gates = _pack_w_in(w_in)
    w_branch16 = w_branch.astype(BF16)
    w_out16 = w_out.astype(BF16)
    for l in range(depth):
        lam_init = 0.8 - 0.6 * math.exp(-0.3 * l)
        wq, wk, p_kr, wv = _pack_mla(mla_w_uq[l], mla_w_ukv[l])

        h = rmsnorm_rows(x, norm_mix[l], BF16)
        slab = matmul(h, w_slab, l, F32, name="proj_in")
        gates = matmul(h, w_gates, l, BF16, act="sigmoid", name="proj_gates")
        qa, ka, ka_bf, va, va_bf, qb, ckv, kr = mixer_prep(slab, tabs, mla_q_norm[l], wq, mla_kv_norm[l])

        diff_kw = dict(diff=True, lamp=diff_lambda[l], subln=diff_subln[l], lam_init=lam_init)
        oa = attention(qa, ka_bf, va_bf, zero_branch, n_streams=bp, q_row0=0, t_q=tp, t_k=tp, q_off=0,
                       tq=ATTN_TQ, tk=ATTN_TK, name="diff_attn_p", **diff_kw)
        oa = attention_cached(qa, ka_bf, va_bf, cache_diff_k[l].reshape(bs, past, 1024),
                              cache_diff_v[l].reshape(bs, past, 1024), oa, row0=n_p,
                              lamp=diff_lambda[l], subln=diff_subln[l], lam_init=lam_init)

        kb, vb = mla_expand(ckv, kr, wk, p_kr, wv)
        ob = attention(qb, kb, vb, zero_branch, n_streams=bp, q_row0=0, t_q=tp, t_k=tp, q_off=0,
                       tq=ATTN_TQ, tk=ATTN_TK, diff=False, name="mla_attn_p")
        ckv_all = jnp.concatenate([cache_mla_ckv[l], ckv[n_p:].reshape(bs, ts, KV_LORA)], axis=1)
        kr_all = jnp.concatenate([cache_mla_krope[l], kr[n_p:].reshape(bs, ts, ROPE_B)], axis=1)
        kb_s, vb_s = mla_expand(ckv_all.reshape(bs * tk_s, KV_LORA), kr_all.reshape(bs * tk_s, ROPE_B),
                                wk, p_kr, wv)
        ob = attention(qb, kb_s, vb_s, ob, n_streams=bs, q_row0=n_p, t_q=ts, t_k=tk_s, q_off=past,
                       tq=ts, tk=tk_s, diff=False, name="mla_attn_s")

        alog_row = _misc_row(gdn_a_log[l], MISC_A0)
        dtb_row = _misc_row(gdn_dt_bias[l], MISC_A0)
        oc, s_p = gated_delta(slab, zero_conv, zero_state, gdn_conv[l], alog_row, dtb_row, gdn_norm[l], zero_branch,
                              row_blk0=0, n_streams=bp, n_chunks=tp // CHUNK)
        oc, s_s = gated_delta(slab, state_gdn_conv[l], state_gdn_s[l], gdn_conv[l], alog_row, dtb_row,
                              gdn_norm[l], oc, row_blk0=n_p // CHUNK, n_streams=bs, n_chunks=ts // CHUNK)
        conv_p = slab[:n_p].reshape(bp, tp, SLAB_F)[:, tp - (CONV_W - 1):, COL_CQ:COL_CQ + QKV_C]
        conv_s = slab[n_p:].reshape(bs, ts, SLAB_F)[:, ts - (CONV_W - 1):, COL_CQ:COL_CQ + QKV_C]

        merged = branch_merge(oa, ob, oc, w_branch16, l, gates)
        x = matmul(merged, w_out16, l, F32, residual=x, name="proj_out")

        w_router = jnp.concatenate([router_group[l], router_expert[l],
                                    jnp.zeros((D_MODEL, LANES - N_GROUPS - N_EXPERTS), F32)], axis=1).astype(BF16)
        b_router = jnp.concatenate([router_group_bias[l], router_expert_bias[l],
                                    jnp.zeros((LANES - N_GROUPS - N_EXPERTS,), F32)]).reshape(1, LANES)
        x = hier_moe_residual(x, norm_ffn[l], w_router, b_router, expert_w_gate, expert_w_up, expert_w_down, l)

        st_p.append((ka[:n_p].reshape(bp, tp, H_A, 2, DH_A), va[:n_p].reshape(bp, tp, H_A, VA),
                     ckv[:n_p].reshape(bp, tp, KV_LORA), kr[:n_p].reshape(bp, tp, ROPE_B), conv_p, s_p))
        st_s.append((ka[n_p:].reshape(bs, ts, H_A, 2, DH_A), va[n_p:].reshape(bs, ts, H_A, VA),
                     ckv[n_p:].reshape(bs, ts, KV_LORA), kr[n_p:].reshape(bs, ts, ROPE_B), conv_s, s_s))

    y_p, y_s = rmsnorm_split(x, norm_final, n_p)
    y_prompt = y_p.reshape(bp, tp, D_MODEL)
    y_sample = y_s.reshape(bs, ts, D_MODEL)

    def stk(sts, i):
        return jnp.stack([s[i] for s in sts])

    return (y_prompt, y_sample,
            stk(st_p, 0), stk(st_p, 1), stk(st_p, 2), stk(st_p, 3), stk(st_p, 4), stk(st_p, 5),
            stk(st_s, 0), stk(st_s, 1), stk(st_s, 2), stk(st_s, 3), stk(st_s, 4), stk(st_s, 5))
```

```python
import functools
import math

import jax
import jax.numpy as jnp
import numpy as np
from jax import lax
from jax.experimental import pallas as pl
from jax.experimental.pallas import tpu as pltpu

F32 = jnp.float32
BF16 = jnp.bfloat16

D_MODEL = 2048
CHUNK = 64
CHUNK_SHIFT = 6
ROPE_THETA = 500000.0
EPS = 1e-6
N_BRANCH = 3
H_A = 8
DH_A = 64
VA = 2 * DH_A
ROT_A = DH_A // 4
H_B = 8
Q_LORA = 512
KV_LORA = 256
NOPE_B = 64
ROPE_B = 32
V_B = 128
H_C = 8
DK_C = 128
DV_C = 128
CONV_W = 4
QKV_C = 2 * H_C * DK_C + H_C * DV_C
BRANCH_W = H_A * VA
N_GROUPS = 4
EXPERTS_PER_GROUP = 4
N_EXPERTS = N_GROUPS * EXPERTS_PER_GROUP
D_EXPERT = 512

LANES = 128
VMEM_LIMIT_BYTES = 48 * 1024 * 1024
EXPERT_VMEM_LIMIT_BYTES = 56 * 1024 * 1024
NEG_BIG = -1e30

COL_AQ, COL_AK, COL_AV = 0, 1024, 2048
COL_DQ, COL_DKV, COL_MISC = 3072, 3584, 3840
COL_CQ, COL_CK, COL_CV, COL_CZ = 4096, 5120, 6144, 7168
SLAB_F = 8192
MISC_B0 = ROPE_B
MISC_A0 = ROPE_B + H_C

MOE_ROWS = 256
MM_ROWS = 1088
SLAB_ROWS = D_MODEL // LANES
ATTN_TQ = 512
ATTN_TK = 512


def _cparams(*sem):
    return pltpu.CompilerParams(dimension_semantics=sem, vmem_limit_bytes=VMEM_LIMIT_BYTES)


def _sigmoid(x):
    return 1.0 / (1.0 + jnp.exp(-x))


def _silu(x):
    return x * _sigmoid(x)


def _rmsnorm_kernel(x_ref, w_ref, o_ref):
    x = x_ref[...]
    ms = jnp.mean(x * x, axis=-1, keepdims=True)
    o_ref[...] = (x * lax.rsqrt(ms + EPS) * w_ref[...]).astype(o_ref.dtype)


def rmsnorm_rows(x, w, out_dtype, tm=512):
    m, d = x.shape
    return pl.pallas_call(
        _rmsnorm_kernel,
        grid=(m // tm,),
        in_specs=[pl.BlockSpec((tm, d), lambda i: (i, 0)),
                  pl.BlockSpec((1, d), lambda i: (0, 0))],
        out_specs=pl.BlockSpec((tm, d), lambda i: (i, 0)),
        out_shape=jax.ShapeDtypeStruct((m, d), out_dtype),
        compiler_params=_cparams("parallel"),
        name="rmsnorm",
    )(x, w.reshape(1, d))


def _rmsnorm_split_kernel(x_ref, w_ref, o1_ref, o2_ref, *, n1):
    x = x_ref[...]
    y = x * lax.rsqrt(jnp.mean(x * x, axis=-1, keepdims=True) + EPS) * w_ref[...]
    i = pl.program_id(0)

    @pl.when(i < n1)
    def _():
        o1_ref[...] = y

    @pl.when(i >= n1)
    def _():
        o2_ref[...] = y


def rmsnorm_split(x, w, rows1, tm=512):
    m, d = x.shape
    n1 = rows1 // tm
    return pl.pallas_call(
        functools.partial(_rmsnorm_split_kernel, n1=n1),
        grid=(m // tm,),
        in_specs=[pl.BlockSpec((tm, d), lambda i: (i, 0)),
                  pl.BlockSpec((1, d), lambda i: (0, 0))],
        out_specs=[pl.BlockSpec((tm, d), lambda i: (jnp.minimum(i, n1 - 1), 0)),
                   pl.BlockSpec((tm, d), lambda i: (jnp.maximum(i - n1, 0), 0))],
        out_shape=[jax.ShapeDtypeStruct((rows1, d), F32), jax.ShapeDtypeStruct((m - rows1, d), F32)],
        compiler_params=_cparams("arbitrary"),
        name="rmsnorm_final",
    )(x, w.reshape(1, d))


def _mm_kernel(a_ref, b_ref, o_ref, *, act):
    acc = jnp.dot(a_ref[...], b_ref[...], preferred_element_type=F32)
    if act == "sigmoid":
        acc = _sigmoid(acc)
    o_ref[...] = acc.astype(o_ref.dtype)


def _mm_res_kernel(a_ref, b_ref, r_ref, o_ref):
    acc = jnp.dot(a_ref[...], b_ref[...], preferred_element_type=F32)
    o_ref[...] = r_ref[...] + acc


def matmul(a, b, layer, out_dtype, act=None, residual=None, tm=MM_ROWS, tn=1024, name="mm"):
    m, k = a.shape
    n = b.shape[2]
    in_specs = [pl.BlockSpec((tm, k), lambda j, i: (i, 0)),
                pl.BlockSpec((None, k, tn), lambda j, i: (layer, 0, j))]
    args = [a, b]
    if residual is None:
        body = functools.partial(_mm_kernel, act=act)
    else:
        body = _mm_res_kernel
        in_specs.append(pl.BlockSpec((tm, tn), lambda j, i: (i, j)))
        args.append(residual)
    return pl.pallas_call(
        body,
        grid=(n // tn, m // tm),
        in_specs=in_specs,
        out_specs=pl.BlockSpec((tm, tn), lambda j, i: (i, j)),
        out_shape=jax.ShapeDtypeStruct((m, n), out_dtype),
        compiler_params=_cparams("parallel", "parallel"),
        name=name,
    )(*args)


def _rope_tables(pos, period, offset, half):
    rot = 2 * half
    inv = ROPE_THETA ** (-jnp.arange(half, dtype=F32) * 2.0 / rot)
    ang = pos.astype(F32)[:, None] * inv[None, :]
    cos, sin = jnp.cos(ang), jnp.sin(ang)
    lane = jnp.arange(LANES)
    r = (lane % period) - offset
    first = (r >= 0) & (r < half)
    second = (r >= half) & (r < rot)
    idx = jnp.clip(jnp.where(second, r - half, r), 0, half - 1)
    cos_l, sin_l = cos[:, idx], sin[:, idx]
    c = jnp.where((first | second)[None, :], cos_l, 1.0)
    sa = jnp.where(first[None, :], -sin_l, 0.0)
    sb = jnp.where(second[None, :], sin_l, 0.0)
    return c, sa, sb


def _rope_lanes(x, c, sa, sb, half):
    return (x * c + pltpu.roll(x, LANES - half, axis=1) * sa
            + pltpu.roll(x, half, axis=1) * sb)


def _prep_kernel(aq_ref, ak_ref, av_ref, dq_ref, dkv_ref, misc_ref,
                 ca_ref, saa_ref, sba_ref, cb_ref, sab_ref, sbb_ref, ck_ref, sak_ref, sbk_ref,
                 qn_ref, wuq_ref, kvn_ref,
                 qa_o, ka_o, kabf_o, va_o, vabf_o, qb_o, ckv_o, kr_o):
    ca, saa, sba = ca_ref[...], saa_ref[...], sba_ref[...]
    for c in range(H_A):
        sl = slice(c * LANES, (c + 1) * LANES)
        q = _rope_lanes(aq_ref[:, sl], ca, saa, sba, ROT_A // 2)
        qa_o[:, sl] = (q * (DH_A ** -0.5)).astype(BF16)
        k = _rope_lanes(ak_ref[:, sl], ca, saa, sba, ROT_A // 2)
        ka_o[:, sl] = k
        kabf_o[:, sl] = k.astype(BF16)
    v = av_ref[...]
    va_o[...] = v
    vabf_o[...] = v.astype(BF16)

    x = dq_ref[...]
    cq = x * lax.rsqrt(jnp.mean(x * x, axis=-1, keepdims=True) + EPS) * qn_ref[...]
    qb = jnp.dot(cq.astype(BF16), wuq_ref[...], preferred_element_type=F32)
    cb, sab, sbb = cb_ref[...], sab_ref[...], sbb_ref[...]
    scale_b = (NOPE_B + ROPE_B) ** -0.5
    for c in range(H_B):
        sl = slice(c * LANES, (c + 1) * LANES)
        q = _rope_lanes(qb[:, sl], cb, sab, sbb, ROPE_B // 2)
        qb_o[:, sl] = (q * scale_b).astype(BF16)

    x = dkv_ref[...]
    ckv_o[...] = x * lax.rsqrt(jnp.mean(x * x, axis=-1, keepdims=True) + EPS) * kvn_ref[...]
    kr = _rope_lanes(misc_ref[...], ck_ref[...], sak_ref[...], sbk_ref[...], ROPE_B // 2)
    kr_o[...] = kr[:, :ROPE_B]


def mixer_prep(slab, tabs, q_norm, w_uq_pad, kv_norm, tm=256):
    m = slab.shape[0]

    def col(width, start):
        blk = start // width
        return pl.BlockSpec((tm, width), lambda i: (i, blk))

    def row(width):
        return pl.BlockSpec((tm, width), lambda i: (i, 0))

    def full(shape):
        return pl.BlockSpec(shape, lambda i: (0,) * len(shape))

    in_specs = [col(1024, COL_AQ), col(1024, COL_AK), col(1024, COL_AV),
                col(Q_LORA, COL_DQ), col(KV_LORA, COL_DKV), col(LANES, COL_MISC)]
    in_specs += [row(LANES)] * 9
    in_specs += [full((1, Q_LORA)), full((Q_LORA, 1024)), full((1, KV_LORA))]
    out_shape = [jax.ShapeDtypeStruct((m, 1024), BF16),
                 jax.ShapeDtypeStruct((m, 1024), F32),
                 jax.ShapeDtypeStruct((m, 1024), BF16),
                 jax.ShapeDtypeStruct((m, 1024), F32),
                 jax.ShapeDtypeStruct((m, 1024), BF16),
                 jax.ShapeDtypeStruct((m, 1024), BF16),
                 jax.ShapeDtypeStruct((m, KV_LORA), F32),
                 jax.ShapeDtypeStruct((m, ROPE_B), F32)]
    out_specs = [row(1024)] * 6 + [row(KV_LORA), row(ROPE_B)]
    return pl.pallas_call(
        _prep_kernel,
        grid=(m // tm,),
        in_specs=in_specs,
        out_specs=out_specs,
        out_shape=out_shape,
        compiler_params=_cparams("parallel"),
        name="mixer_prep",
    )(slab, slab, slab, slab, slab, slab, *tabs,
      q_norm.reshape(1, Q_LORA), w_uq_pad, kv_norm.reshape(1, KV_LORA))


def _expand_kernel(ckv_ref, kr_ref, wuk_ref, pk_ref, wuv_ref, kb_o, vb_o):
    ckv = ckv_ref[...].astype(BF16)
    kr = kr_ref[...].astype(BF16)
    kb = (jnp.dot(ckv, wuk_ref[...], preferred_element_type=F32)
          + jnp.dot(kr, pk_ref[...], preferred_element_type=F32))
    kb_o[...] = kb.astype(BF16)
    vb_o[...] = jnp.dot(ckv, wuv_ref[...], preferred_element_type=F32).astype(BF16)


def mla_expand(ckv, kr, w_uk_pad, p_kr, w_uv, tm=512):
    m = ckv.shape[0]
    return pl.pallas_call(
        _expand_kernel,
        grid=(m // tm,),
        in_specs=[pl.BlockSpec((tm, KV_LORA), lambda i: (i, 0)),
                  pl.BlockSpec((tm, ROPE_B), lambda i: (i, 0)),
                  pl.BlockSpec((KV_LORA, 1024), lambda i: (0, 0)),
                  pl.BlockSpec((ROPE_B, 1024), lambda i: (0, 0)),
                  pl.BlockSpec((KV_LORA, 1024), lambda i: (0, 0))],
        out_specs=[pl.BlockSpec((tm, 1024), lambda i: (i, 0))] * 2,
        out_shape=[jax.ShapeDtypeStruct((m, 1024), BF16)] * 2,
        compiler_params=_cparams("parallel"),
        name="mla_expand",
    )(ckv, kr, w_uk_pad, p_kr, w_uv)


def _attn_kernel(*refs, diff, tq, tk, q_off, nk, lam_init, aliased, group):
    q_ref, k_ref, v_ref = refs[:3]
    pos = 3
    if diff:
        lamp_ref, subln_ref = refs[3:5]
        pos = 5
    if aliased:
        pos += 1
    single = nk == 1
    if single:
        o_ref, bias_sc = refs[pos:pos + 2]
    else:
        o_ref, bias_sc, m_sc, l_sc, acc_sc = refs[pos:pos + 5]
    n_maps = 2 if diff else 1
    maps = [(h, mp) for h in range(H_A) for mp in range(n_maps)]
    i = pl.program_id(1)
    j = pl.program_id(2)

    def write_head(h, outs):
        if diff:
            lp = lamp_ref[...]
            lam = (jnp.exp(jnp.sum(lp[0:1] * lp[1:2], axis=-1, keepdims=True))
                   - jnp.exp(jnp.sum(lp[2:3] * lp[3:4], axis=-1, keepdims=True)) + lam_init)
            o = outs[0] - lam * outs[1]
            o = o * lax.rsqrt(jnp.mean(o * o, axis=-1, keepdims=True) + EPS) * subln_ref[...]
            o = o * (1.0 - lam_init)
        else:
            o = outs[0]
        o_ref[:, h * LANES:(h + 1) * LANES] = o.astype(o_ref.dtype)

    if not single:
        @pl.when(j == 0)
        def _():
            m_sc[...] = jnp.full(m_sc.shape, NEG_BIG, F32)
            l_sc[...] = jnp.zeros(l_sc.shape, F32)
            acc_sc[...] = jnp.zeros(acc_sc.shape, F32)

    q_chunk_max = (q_off + i * tq + tq - 1) // CHUNK

    @pl.when((j * tk) // CHUNK <= q_chunk_max)
    def _():
        q_chunk = lax.shift_right_logical(q_off + i * tq + lax.broadcasted_iota(jnp.int32, (tq, 1), 0),
                                          CHUNK_SHIFT)
        k_chunk = lax.shift_right_logical(j * tk + lax.broadcasted_iota(jnp.int32, (1, tk), 1), CHUNK_SHIFT)
        bias_sc[...] = jnp.where(q_chunk >= k_chunk, 0.0, NEG_BIG)
        lane = lax.broadcasted_iota(jnp.int32, (1, LANES), 1)

        def scores(h, mp):
            hs = slice(h * LANES, (h + 1) * LANES)
            q = q_ref[:, hs]
            if diff:
                in_map = (lane < DH_A) if mp == 0 else (lane >= DH_A)
                q = jnp.where(in_map, q, jnp.zeros_like(q))
            return lax.dot_general(q, k_ref[:, hs], (((1,), (1,)), ((), ())),
                                   preferred_element_type=F32) + bias_sc[...]

        def pv(p, h):
            return jnp.dot(p.astype(BF16), v_ref[:, h * LANES:(h + 1) * LANES], preferred_element_type=F32)

        for g0 in range(0, len(maps), group):
            grp = maps[g0:g0 + group]
            s = [scores(h, mp) for h, mp in grp]
            m_cur = [jnp.max(x, axis=-1, keepdims=True) for x in s]
            if single:
                p = [jnp.exp(x - m) for x, m in zip(s, m_cur)]
                l = [jnp.sum(x, axis=-1, keepdims=True) for x in p]
                outs = [pv(x, h) / d for x, d, (h, _) in zip(p, l, grp)]
                for t in range(0, len(grp), n_maps):
                    write_head(grp[t][0], outs[t:t + n_maps])
            else:
                ids = [h * n_maps + mp for h, mp in grp]
                m_prev = [m_sc[t] for t in ids]
                m_new = [jnp.maximum(a, b) for a, b in zip(m_prev, m_cur)]
                p = [jnp.exp(x - jnp.concatenate([m] * (tk // LANES), axis=1)) for x, m in zip(s, m_new)]
                alpha = [jnp.exp(a - b) for a, b in zip(m_prev, m_new)]
                for t, a, x, m, (h, _) in zip(ids, alpha, p, m_new, grp):
                    l_sc[t] = a * l_sc[t] + jnp.sum(x, axis=-1, keepdims=True)
                    acc_sc[t] = a * acc_sc[t] + pv(x, h)
                    m_sc[t] = m

    if not single:
        @pl.when(j == nk - 1)
        def _():
            for h in range(H_A):
                write_head(h, [acc_sc[h * n_maps + mp] / l_sc[h * n_maps + mp] for mp in range(n_maps)])


def attention(q, k, v, out_prev, *, n_streams, q_row0, t_q, t_k, q_off, tq, tk, diff,
              lamp=None, subln=None, lam_init=0.0, group=2, name="attn"):
    nq, nk = t_q // tq, t_k // tk
    qb0 = q_row0 // tq
    width = H_A * LANES

    def q_map(b, i, j):
        return (qb0 + b * nq + i, 0)

    def kv_map(b, i, j):
        last = ((q_off + i * tq + tq - 1) // CHUNK * CHUNK) // tk
        return (b * nk + jnp.minimum(j, last), 0)

    in_specs = [pl.BlockSpec((tq, width), q_map),
                pl.BlockSpec((tk, width), kv_map),
                pl.BlockSpec((tk, width), kv_map)]
    args = [q, k, v]
    n_maps = 1
    if diff:
        n_maps = 2
        in_specs += [pl.BlockSpec((4, DH_A), lambda b, i, j: (0, 0)),
                     pl.BlockSpec((1, VA), lambda b, i, j: (0, 0))]
        args += [lamp, subln.reshape(1, VA)]
    aliases = {}
    if out_prev is not None:
        aliases = {len(args): 0}
        in_specs.append(pl.BlockSpec(memory_space=pl.ANY))
        args.append(out_prev)
    return pl.pallas_call(
        functools.partial(_attn_kernel, diff=diff, tq=tq, tk=tk, q_off=q_off, nk=nk, lam_init=lam_init,
                          aliased=out_prev is not None, group=group),
        grid=(n_streams, nq, nk),
        in_specs=in_specs,
        out_specs=pl.BlockSpec((tq, width), q_map),
        out_shape=jax.ShapeDtypeStruct((q.shape[0], width), BF16),
        scratch_shapes=[pltpu.VMEM((tq, tk), F32)] + ([] if nk == 1 else
                                                      [pltpu.VMEM((H_A * n_maps, tq, LANES), F32)] * 3),
        input_output_aliases=aliases,
        compiler_params=_cparams("parallel", "parallel", "arbitrary"),
        name=name,
    )(*args)


def _attn_cached_kernel(q_ref, kn_ref, vn_ref, kc_ref, vc_ref, lamp_ref, subln_ref, prev_ref, o_ref, bias_sc,
                        *, ts, past, lam_init):
    del prev_ref
    pos = past + lax.broadcasted_iota(jnp.int32, (ts, 1), 0)
    kpos = past + lax.broadcasted_iota(jnp.int32, (1, ts), 1)
    bias_sc[...] = jnp.where(lax.shift_right_logical(pos, CHUNK_SHIFT)
                             >= lax.shift_right_logical(kpos, CHUNK_SHIFT), 0.0, NEG_BIG)
    lane = lax.broadcasted_iota(jnp.int32, (1, LANES), 1)
    lp = lamp_ref[...]
    lam = (jnp.exp(jnp.sum(lp[0:1] * lp[1:2], axis=-1, keepdims=True))
           - jnp.exp(jnp.sum(lp[2:3] * lp[3:4], axis=-1, keepdims=True)) + lam_init)
    nt = (((1,), (1,)), ((), ()))
    for h in range(H_A):
        hs = slice(h * LANES, (h + 1) * LANES)
        q = q_ref[:, hs]
        qs = [jnp.where(lane < DH_A, q, jnp.zeros_like(q)), jnp.where(lane >= DH_A, q, jnp.zeros_like(q))]
        kc = kc_ref[0, :, hs].astype(BF16)
        vc = vc_ref[0, :, hs].astype(BF16)
        kn = kn_ref[:, hs]
        vn = vn_ref[:, hs]
        s_old = [lax.dot_general(x, kc, nt, preferred_element_type=F32) for x in qs]
        s_new = [lax.dot_general(x, kn, nt, preferred_element_type=F32) + bias_sc[...] for x in qs]
        m = [jnp.maximum(jnp.max(a, axis=-1, keepdims=True), jnp.max(b, axis=-1, keepdims=True))
             for a, b in zip(s_old, s_new)]
        p_old = [jnp.exp(a - t) for a, t in zip(s_old, m)]
        p_new = [jnp.exp(b - t) for b, t in zip(s_new, m)]
        l = [jnp.sum(a, axis=-1, keepdims=True) + jnp.sum(b, axis=-1, keepdims=True)
             for a, b in zip(p_old, p_new)]
        outs = [(jnp.dot(a.astype(BF16), vc, preferred_element_type=F32)
                 + jnp.dot(b.astype(BF16), vn, preferred_element_type=F32)) / d
                for a, b, d in zip(p_old, p_new, l)]
        o = outs[0] - lam * outs[1]
        o = o * lax.rsqrt(jnp.mean(o * o, axis=-1, keepdims=True) + EPS) * subln_ref[...]
        o_ref[:, hs] = (o * (1.0 - lam_init)).astype(o_ref.dtype)


def attention_cached(q, k_new, v_new, cache_k, cache_v, out_prev, *, row0, lamp, subln, lam_init):
    bs, past, width = cache_k.shape
    ts = (q.shape[0] - row0) // bs
    blk0 = row0 // ts
    new_spec = pl.BlockSpec((ts, width), lambda b: (blk0 + b, 0))
    cache_spec = pl.BlockSpec((1, past, width), lambda b: (b, 0, 0))
    return pl.pallas_call(
        functools.partial(_attn_cached_kernel, ts=ts, past=past, lam_init=lam_init),
        grid=(bs,),
        in_specs=[new_spec, new_spec, new_spec, cache_spec, cache_spec,
                  pl.BlockSpec((4, DH_A), lambda b: (0, 0)),
                  pl.BlockSpec((1, VA), lambda b: (0, 0)),
                  pl.BlockSpec(memory_space=pl.ANY)],
        out_specs=new_spec,
        out_shape=jax.ShapeDtypeStruct(out_prev.shape, BF16),
        scratch_shapes=[pltpu.VMEM((ts, ts), F32)],
        input_output_aliases={7: 0},
        compiler_params=_cparams("parallel"),
        name="diff_attn_s",
    )(q, k_new, v_new, cache_k, cache_v, lamp, subln.reshape(1, VA), out_prev)


def _mm_nt(a, b):
    return lax.dot_general(a, b, (((1,), (1,)), ((), ())), preferred_element_type=F32)


def _mm_nn(a, b):
    return jnp.dot(a, b, preferred_element_type=F32)


def _mm_tn(a, b):
    return lax.dot_general(a, b, (((0,), (0,)), ((), ())), preferred_element_type=F32)


def _bf(xs):
    return [x.astype(BF16) for x in xs]


def _shift_rows(x, tail, s):
    xs = pltpu.roll(x, s, axis=0)
    ts = pltpu.roll(tail, s, axis=0)
    row = lax.broadcasted_iota(jnp.int32, (8, 1), 0)
    top = jnp.where(row < s, ts, xs[0:8])
    return jnp.concatenate([top, xs[8:]], axis=0)


def _gdn_kernel(*refs, n_chunks, aliased):
    (cq_ref, ck_ref, cv_ref, cz_ref, misc_ref, convp_ref, s0_ref,
     kern_ref, alog_ref, dtb_ref, norm_ref) = refs[:11]
    o_ref, s_o, s_sc, tail_sc = refs[(12 if aliased else 11):]
    n = pl.program_id(1)
    L = CHUNK
    heads = range(H_C)

    @pl.when(n == 0)
    def _():
        s_sc[...] = s0_ref[0]
        tail_sc[...] = jnp.zeros(tail_sc.shape, F32)
        tail_sc[8 - (CONV_W - 1):8, :] = convp_ref[0]

    conv = []
    for part, ref in enumerate((cq_ref, ck_ref, cv_ref)):
        cs = slice(part * 1024, (part + 1) * 1024)
        x = ref[...]
        tail = tail_sc[:, cs]
        acc = x * kern_ref[CONV_W - 1:CONV_W, cs]
        for s in range(1, CONV_W):
            acc = acc + _shift_rows(x, tail, s) * kern_ref[CONV_W - 1 - s:CONV_W - s, cs]
        conv.append(_silu(acc))
        tail_sc[:, cs] = x[L - 8:L]
    qc, kc, vc = conv

    misc = misc_ref[...]
    beta_all = _sigmoid(misc)
    za = misc + dtb_ref[...]
    softplus = jnp.maximum(za, 0.0) + jnp.log(1.0 + jnp.exp(-jnp.abs(za)))
    g_all = -jnp.exp(alog_ref[...]) * softplus
    row = lax.broadcasted_iota(jnp.int32, (L, 1), 0)
    gcum_all = g_all
    for s in (1, 2, 4, 8, 16, 32):
        gcum_all = gcum_all + jnp.where(row >= s, pltpu.roll(gcum_all, s, axis=0), 0.0)
    lane = lax.broadcasted_iota(jnp.int32, (1, LANES), 1)
    is_beta = (lane >= MISC_B0) & (lane < MISC_B0 + H_C)
    packed = jnp.where(is_beta, beta_all, gcum_all)
    packed_t = jnp.concatenate([packed, jnp.zeros((LANES - L, LANES), F32)], axis=0).T

    ri = lax.broadcasted_iota(jnp.int32, (L, L), 0)
    ci = lax.broadcasted_iota(jnp.int32, (L, L), 1)
    incl = ri >= ci
    strict = ri > ci
    diag16 = (ri // 16) == (ci // 16)

    qn, kn, v_h, beta, gcum, decay, gam = [], [], [], [], [], [], []
    for h in heads:
        hs = slice(h * LANES, (h + 1) * LANES)
        qh, kh = qc[:, hs], kc[:, hs]
        qn.append(qh * lax.rsqrt(jnp.sum(qh * qh, axis=-1, keepdims=True) + EPS) * (DK_C ** -0.5))
        kn.append(kh * lax.rsqrt(jnp.sum(kh * kh, axis=-1, keepdims=True) + EPS))
        v_h.append(vc[:, hs])
        beta.append(beta_all[:, MISC_B0 + h:MISC_B0 + h + 1])
        gc = gcum_all[:, MISC_A0 + h:MISC_A0 + h + 1]
        gcum.append(gc)
        seg = gc - packed_t[MISC_A0 + h:MISC_A0 + h + 1, 0:L]
        decay.append(jnp.where(incl, jnp.exp(jnp.where(incl, seg, 0.0)), 0.0))
        gam.append(jnp.exp(gc))
    q16, k16 = _bf(qn), _bf(kn)
    kk = [_mm_nt(k16[h], k16[h]) for h in heads]
    pq = [_mm_nt(q16[h], k16[h]) * decay[h] for h in heads]
    a_mat = [jnp.where(strict, beta[h] * kk[h] * decay[h], 0.0) for h in heads]

    a_d = [jnp.where(diag16, a_mat[h], 0.0) for h in heads]
    a_lo = [a_mat[h] - a_d[h] for h in heads]
    ad16 = _bf(a_d)
    a2 = [_mm_nn(ad16[h], ad16[h]) for h in heads]
    a2_16 = _bf(a2)
    a4 = [_mm_nn(a2_16[h], a2_16[h]) for h in heads]
    a4_16 = _bf(a4)
    a8 = [_mm_nn(a4_16[h], a4_16[h]) for h in heads]
    f12 = [a2[h] - a_d[h] - _mm_nn(ad16[h], a2_16[h]) for h in heads]
    a8_16 = _bf(a8)
    f34 = [a4[h] + a8[h] + _mm_nn(a4_16[h], a8_16[h]) for h in heads]
    f12_16, f34_16 = _bf(f12), _bf(f34)
    g = [f12[h] + f34[h] + _mm_nn(f12_16[h], f34_16[h]) for h in heads]
    g16, alo16 = _bf(g), _bf(a_lo)
    nm = [a_lo[h] + _mm_nn(g16[h], alo16[h]) for h in heads]
    n16 = _bf(nm)
    n2 = [_mm_nn(n16[h], n16[h]) for h in heads]
    n2_16 = _bf(n2)
    hm = [n2[h] - nm[h] - _mm_nn(n16[h], n2_16[h]) for h in heads]
    hm16 = _bf(hm)
    tm1 = [hm[h] + g[h] + _mm_nn(hm16[h], g16[h]) for h in heads]
    tm1_16 = _bf(tm1)
    rhs = [jnp.concatenate([v_h[h] * beta[h], kn[h] * (beta[h] * gam[h])], axis=-1) for h in heads]
    rhs16 = _bf(rhs)
    sol = [rhs[h] + _mm_nn(tm1_16[h], rhs16[h]) for h in heads]

    s_prev = [s_sc[h] for h in heads]
    s16 = _bf(s_prev)
    w16 = _bf([sol[h][:, DV_C:] for h in heads])
    u = [sol[h][:, :DV_C] - _mm_nt(w16[h], s16[h]) for h in heads]
    u16 = _bf(u)
    pq16 = _bf(pq)
    o = [gam[h] * _mm_nt(q16[h], s16[h]) + _mm_nn(pq16[h], u16[h]) for h in heads]
    for h in heads:
        gcum_end = gcum[h][L - 1:L, :]
        ut = (u[h] * jnp.exp(gcum_end - gcum[h])).astype(BF16)
        s_sc[h] = jnp.exp(gcum_end) * s_prev[h] + _mm_tn(ut, k16[h])
    for h in heads:
        hs = slice(h * LANES, (h + 1) * LANES)
        oh = o[h] * lax.rsqrt(jnp.mean(o[h] * o[h], axis=-1, keepdims=True) + EPS) * norm_ref[...]
        o_ref[:, hs] = (oh * _silu(cz_ref[:, hs])).astype(o_ref.dtype)

    @pl.when(n == n_chunks - 1)
    def _():
        s_o[0] = s_sc[...]


def gated_delta(slab, conv_prev, s0, kern, alog_row, dtb_row, norm_w, out_prev, *, row_blk0, n_streams, n_chunks):
    def rows(b, n):
        return row_blk0 + b * n_chunks + n

    def col(start):
        blk = start // 1024
        return pl.BlockSpec((CHUNK, 1024), lambda b, n: (rows(b, n), blk))

    def full(shape):
        return pl.BlockSpec(shape, lambda b, n: (0,) * len(shape))

    in_specs = [col(COL_CQ), col(COL_CK), col(COL_CV), col(COL_CZ),
                pl.BlockSpec((CHUNK, LANES), lambda b, n: (rows(b, n), COL_MISC // LANES)),
                pl.BlockSpec((1, CONV_W - 1, QKV_C), lambda b, n: (b, 0, 0)),
                pl.BlockSpec((1, H_C, DV_C, DK_C), lambda b, n: (b, 0, 0, 0)),
                full((CONV_W, QKV_C)), full((1, LANES)), full((1, LANES)), full((1, DV_C))]
    args = [slab, slab, slab, slab, slab, conv_prev, s0, kern, alog_row, dtb_row, norm_w.reshape(1, DV_C)]
    aliases = {}
    if out_prev is not None:
        aliases = {len(args): 0}
        in_specs.append(pl.BlockSpec(memory_space=pl.ANY))
        args.append(out_prev)
    return pl.pallas_call(
        functools.partial(_gdn_kernel, n_chunks=n_chunks, aliased=out_prev is not None),
        grid=(n_streams, n_chunks),
        in_specs=in_specs,
        out_specs=[pl.BlockSpec((CHUNK, 1024), lambda b, n: (rows(b, n), 0)),
                   pl.BlockSpec((1, H_C, DV_C, DK_C), lambda b, n: (b, 0, 0, 0))],
        out_shape=[jax.ShapeDtypeStruct((slab.shape[0], 1024), BF16),
                   jax.ShapeDtypeStruct((n_streams, H_C, DV_C, DK_C), F32)],
        scratch_shapes=[pltpu.VMEM((H_C, DV_C, DK_C), F32),
                        pltpu.VMEM((8, QKV_C), F32)],
        input_output_aliases=aliases,
        compiler_params=_cparams("parallel", "arbitrary"),
        name="gated_delta",
    )(*args)


def _merge_kernel(oa_ref, ob_ref, oc_ref, wb_ref, ga_ref, gb_ref, gc_ref, o_ref):
    acc = jnp.dot(oa_ref[...], wb_ref[0], preferred_element_type=F32) * ga_ref[...].astype(F32)
    acc = acc + jnp.dot(ob_ref[...], wb_ref[1], preferred_element_type=F32) * gb_ref[...].astype(F32)
    acc = acc + jnp.dot(oc_ref[...], wb_ref[2], preferred_element_type=F32) * gc_ref[...].astype(F32)
    o_ref[...] = acc.astype(o_ref.dtype)


def branch_merge(oa, ob, oc, w_branch, layer, gates, tm=MM_ROWS // 2, tn=1024):
    m = oa.shape[0]
    nj = D_MODEL // tn

    def gate_spec(nb):
        return pl.BlockSpec((tm, tn), lambda j, i: (i, nb * nj + j))

    return pl.pallas_call(
        _merge_kernel,
        grid=(nj, m // tm),
        in_specs=[pl.BlockSpec((tm, BRANCH_W), lambda j, i: (i, 0))] * 3
                 + [pl.BlockSpec((None, N_BRANCH, BRANCH_W, tn), lambda j, i: (layer, 0, 0, j))]
                 + [gate_spec(0), gate_spec(1), gate_spec(2)],
        out_specs=pl.BlockSpec((tm, tn), lambda j, i: (i, j)),
        out_shape=jax.ShapeDtypeStruct((m, D_MODEL), BF16),
        compiler_params=_cparams("parallel", "parallel"),
        name="branch_merge",
    )(oa, ob, oc, w_branch, gates, gates, gates)


def _slab_store(ref, x, rows):
    for c in range(SLAB_ROWS):
        ref[pl.ds(c, rows, stride=SLAB_ROWS), :] = x[:, c * LANES:(c + 1) * LANES]


def _slab_load(ref, rows):
    return jnp.concatenate([ref[pl.ds(c, rows, stride=SLAB_ROWS), :] for c in range(SLAB_ROWS)], axis=1)


def _router_kernel(x_ref, nw_ref, wr_ref, br_ref, h_o, info_o, cnt_o, cnt_sc, tri_sc, *, tm):
    step = pl.program_id(0)

    @pl.when(step == 0)
    def _():
        cnt_sc[...] = jnp.zeros(cnt_sc.shape, F32)
        ri = lax.broadcasted_iota(jnp.int32, (tm, tm), 0)
        ci = lax.broadcasted_iota(jnp.int32, (tm, tm), 1)
        tri_sc[...] = (ri > ci).astype(BF16)

    x = x_ref[...]
    h = (x * lax.rsqrt(jnp.mean(x * x, axis=-1, keepdims=True) + EPS) * nw_ref[...])
    _slab_store(h_o, h, tm)
    logits = jnp.dot(h.astype(BF16), wr_ref[...], preferred_element_type=F32) + br_ref[...]
    lane = lax.broadcasted_iota(jnp.int32, logits.shape, 1)
    is_g = lane < N_GROUPS
    lg = jnp.where(is_g, logits, NEG_BIG)
    eg = jnp.where(is_g, jnp.exp(lg - jnp.max(lg, axis=-1, keepdims=True)), 0.0)
    gp = eg / jnp.sum(eg, axis=-1, keepdims=True)
    pg = jnp.max(gp, axis=-1, keepdims=True)
    grp = jnp.min(jnp.where(is_g & (gp == pg), lane, LANES), axis=-1, keepdims=True)
    lo = N_GROUPS + grp * EXPERTS_PER_GROUP
    is_e = (lane >= lo) & (lane < lo + EXPERTS_PER_GROUP)
    le = jnp.where(is_e, logits, NEG_BIG)
    ee = jnp.where(is_e, jnp.exp(le - jnp.max(le, axis=-1, keepdims=True)), 0.0)
    ep = ee / jnp.sum(ee, axis=-1, keepdims=True)
    v1 = jnp.max(jnp.where(is_e, ep, -1.0), axis=-1, keepdims=True)
    i1 = jnp.min(jnp.where(is_e & (ep == v1), lane, LANES), axis=-1, keepdims=True)
    rest = is_e & (lane != i1)
    v2 = jnp.max(jnp.where(rest, ep, -1.0), axis=-1, keepdims=True)
    i2 = jnp.min(jnp.where(rest & (ep == v2), lane, LANES), axis=-1, keepdims=True)
    den = v1 + v2
    w1 = pg * v1 / den
    w2 = pg * v2 / den

    oh1 = lane == i1
    oh2 = lane == i2
    picked = (oh1 | oh2).astype(F32)
    before = jnp.dot(tri_sc[...], picked.astype(BF16), preferred_element_type=F32) + cnt_sc[...]
    rank1 = jnp.sum(jnp.where(oh1, before, 0.0), axis=-1, keepdims=True)
    rank2 = jnp.sum(jnp.where(oh2, before, 0.0), axis=-1, keepdims=True)
    cnt = cnt_sc[...] + jnp.sum(picked, axis=0, keepdims=True)
    cnt_sc[...] = cnt
    cnt_o[...] = cnt

    info = jnp.where(lane == 0, (i1 - N_GROUPS).astype(F32), 0.0)
    info = jnp.where(lane == 1, (i2 - N_GROUPS).astype(F32), info)
    info = jnp.where(lane == 2, w1, info)
    info = jnp.where(lane == 3, w2, info)
    info = jnp.where(lane == 4, rank1, info)
    info = jnp.where(lane == 5, rank2, info)
    info_o[...] = info


def moe_router(x, norm_w, w_router, b_router, tm=512):
    m = x.shape[0]
    return pl.pallas_call(
        functools.partial(_router_kernel, tm=tm),
        grid=(m // tm,),
        in_specs=[pl.BlockSpec((tm, D_MODEL), lambda i: (i, 0)),
                  pl.BlockSpec((1, D_MODEL), lambda i: (0, 0)),
                  pl.BlockSpec((D_MODEL, LANES), lambda i: (0, 0)),
                  pl.BlockSpec((1, LANES), lambda i: (0, 0))],
        out_specs=[pl.BlockSpec((tm * SLAB_ROWS, LANES), lambda i: (i, 0)),
                   pl.BlockSpec((tm, LANES), lambda i: (i, 0)),
                   pl.BlockSpec((1, LANES), lambda i: (0, 0))],
        out_shape=[jax.ShapeDtypeStruct((m * SLAB_ROWS, LANES), F32),
                   jax.ShapeDtypeStruct((m, LANES), F32),
                   jax.ShapeDtypeStruct((1, LANES), F32)],
        scratch_shapes=[pltpu.VMEM((1, LANES), F32), pltpu.VMEM((tm, tm), BF16)],
        compiler_params=_cparams("arbitrary"),
        name="moe_router",
    )(x, norm_w.reshape(1, D_MODEL), w_router, b_router)


def _slab_copy(src_ref, dst_ref, sem, tok, r):
    return pltpu.make_async_copy(src_ref.at[pl.ds(tok * SLAB_ROWS, SLAB_ROWS), :],
                                 dst_ref.at[pl.ds(r * SLAB_ROWS, SLAB_ROWS), :], sem)


def _gather_start(src_ref, dst_ref, sem, idx_ref, rows):
    def body(r2, c):
        for u in range(2):
            r = r2 * 2 + u
            _slab_copy(src_ref, dst_ref, sem, idx_ref[0, 0, r], r).start(priority=u)
        return c

    lax.fori_loop(0, rows // 2, body, 0, unroll=4)


def _gather_wait(src_ref, dst_ref, sem, rows):
    def body(r, c):
        _slab_copy(src_ref, dst_ref, sem, 0, r).wait()
        return c

    lax.fori_loop(0, rows, body, 0, unroll=8)


def _combine_kernel(i1c_ref, i2c_ref, i1n_ref, i2n_ref, src_ref, info_ref, res_ref, o_ref, buf1, buf2, sem,
                    *, rows):
    step = pl.program_id(0)
    slot = lax.rem(step, 2)

    def start(i1_ref, i2_ref, s):
        _gather_start(src_ref, buf1.at[s], sem.at[0, s], i1_ref, rows)
        _gather_start(src_ref, buf2.at[s], sem.at[1, s], i2_ref, rows)

    @pl.when(step == 0)
    def _():
        start(i1c_ref, i2c_ref, 0)

    @pl.when(step + 1 < pl.num_programs(0))
    def _():
        start(i1n_ref, i2n_ref, 1 - slot)

    _gather_wait(src_ref, buf1.at[slot], sem.at[0, slot], rows)
    _gather_wait(src_ref, buf2.at[slot], sem.at[1, slot], rows)
    info = info_ref[...]
    o_ref[...] = (res_ref[...] + info[:, 2:3] * _slab_load(buf1.at[slot], rows)
                  + info[:, 3:4] * _slab_load(buf2.at[slot], rows))


def combine_slabs(ys, dest1, dest2, info, residual, rows=MOE_ROWS):
    n = dest1.shape[0]
    nb = n // rows
    cur = pl.BlockSpec((1, 1, rows), lambda i: (i, 0, 0), memory_space=pltpu.SMEM)
    nxt = pl.BlockSpec((1, 1, rows), lambda i: (jnp.minimum(i + 1, nb - 1), 0, 0), memory_space=pltpu.SMEM)
    d1, d2 = dest1.reshape(nb, 1, rows), dest2.reshape(nb, 1, rows)
    return pl.pallas_call(
        functools.partial(_combine_kernel, rows=rows),
        grid=(nb,),
        in_specs=[cur, cur, nxt, nxt, pl.BlockSpec(memory_space=pl.ANY),
                  pl.BlockSpec((rows, LANES), lambda i: (i, 0)),
                  pl.BlockSpec((rows, D_MODEL), lambda i: (i, 0))],
        out_specs=pl.BlockSpec((rows, D_MODEL), lambda i: (i, 0)),
        out_shape=jax.ShapeDtypeStruct((n, D_MODEL), F32),
        scratch_shapes=[pltpu.VMEM((2, rows * SLAB_ROWS, LANES), F32)] * 2 + [pltpu.SemaphoreType.DMA((2, 2))],
        compiler_params=_cparams("arbitrary"),
        name="moe_combine",
    )(d1, d2, d1, d2, ys, info, residual)


def _expert_kernel(be_ref, nu_ref, idx0_ref, idxn_ref, h_ref, wg_ref, wu_ref, wd_ref, o_ref,
                   xbuf, sem, wg_sc, wu_sc, wd_sc, *, rows):
    b = pl.program_id(0)
    n_used = nu_ref[0]
    slot = lax.rem(b, 2)
    prev = be_ref[jnp.maximum(b - 1, 0)]

    @pl.when(b == 0)
    def _():
        _gather_start(h_ref, xbuf.at[0], sem.at[0], idx0_ref, rows)

    @pl.when(b + 1 < n_used)
    def _():
        _gather_start(h_ref, xbuf.at[1 - slot], sem.at[1 - slot], idxn_ref, rows)

    @pl.when((b == 0) | (be_ref[b] != prev))
    def _():
        wg_sc[...] = wg_ref[...].astype(BF16)
        wu_sc[...] = wu_ref[...].astype(BF16)
        wd_sc[...] = wd_ref[...].astype(BF16)

    @pl.when(b < n_used)
    def _():
        _gather_wait(h_ref, xbuf.at[slot], sem.at[slot], rows)
        x = _slab_load(xbuf.at[slot], rows).astype(BF16)
        g = jnp.dot(x, wg_sc[...], preferred_element_type=F32)
        u = jnp.dot(x, wu_sc[...], preferred_element_type=F32)
        hid = _silu(g) * u
        _slab_store(o_ref, jnp.dot(hid.astype(BF16), wd_sc[...], preferred_element_type=F32), rows)

    @pl.when(b >= n_used)
    def _():
        o_ref[...] = jnp.zeros(o_ref.shape, F32)


def routed_experts(h, row_token, block_expert, n_used, w_gate, w_up, w_down, layer, rows=MOE_ROWS):
    n_slots = row_token.shape[0]
    nb = n_slots // rows
    grid_spec = pltpu.PrefetchScalarGridSpec(
        num_scalar_prefetch=2,
        grid=(nb,),
        in_specs=[pl.BlockSpec((1, 1, rows), lambda b, be, nu: (0, 0, 0), memory_space=pltpu.SMEM),
                  pl.BlockSpec((1, 1, rows), lambda b, be, nu: (jnp.minimum(b + 1, nb - 1), 0, 0),
                               memory_space=pltpu.SMEM),
                  pl.BlockSpec(memory_space=pl.ANY),
                  pl.BlockSpec((None, None, D_MODEL, D_EXPERT), lambda b, be, nu: (layer, be[b], 0, 0)),
                  pl.BlockSpec((None, None, D_MODEL, D_EXPERT), lambda b, be, nu: (layer, be[b], 0, 0)),
                  pl.BlockSpec((None, None, D_EXPERT, D_MODEL), lambda b, be, nu: (layer, be[b], 0, 0))],
        out_specs=pl.BlockSpec((rows * SLAB_ROWS, LANES), lambda b, be, nu: (b, 0)),
        scratch_shapes=[pltpu.VMEM((2, rows * SLAB_ROWS, LANES), F32),
                        pltpu.SemaphoreType.DMA((2,)),
                        pltpu.VMEM((D_MODEL, D_EXPERT), BF16),
                        pltpu.VMEM((D_MODEL, D_EXPERT), BF16),
                        pltpu.VMEM((D_EXPERT, D_MODEL), BF16)],
    )
    idx = row_token.reshape(nb, 1, rows)
    return pl.pallas_call(
        functools.partial(_expert_kernel, rows=rows),
        grid_spec=grid_spec,
        out_shape=jax.ShapeDtypeStruct((n_slots * SLAB_ROWS, LANES), F32),
        compiler_params=pltpu.CompilerParams(dimension_semantics=("arbitrary",),
                                             vmem_limit_bytes=EXPERT_VMEM_LIMIT_BYTES),
        name="routed_experts",
    )(block_expert, n_used, idx, idx, h, w_gate, w_up, w_down)


def _routing_plan(e1, e2, rank1, rank2, count, rows):
    n = e1.shape[0]
    n_slots = 2 * n + N_EXPERTS * rows
    padded = (count + rows - 1) // rows * rows
    ends = jnp.cumsum(padded)
    starts = ends - padded
    dest1 = starts[e1] + rank1
    dest2 = starts[e2] + rank2
    tok = jnp.arange(n, dtype=jnp.int32)
    row_token = jnp.zeros((n_slots,), jnp.int32).at[jnp.concatenate([dest1, dest2])].set(
        jnp.concatenate([tok, tok]), unique_indices=True)
    blk_start = jnp.arange(n_slots // rows, dtype=jnp.int32) * rows
    block_expert = jnp.minimum(jnp.sum(ends[None, :] <= blk_start[:, None], axis=1), N_EXPERTS - 1).astype(jnp.int32)
    n_used = (ends[-1] // rows).astype(jnp.int32).reshape(1)
    return row_token, block_expert, n_used, dest1, dest2


def hier_moe_residual(x, norm_w, w_router, b_router, w_gate, w_up, w_down, layer):
    h, info, cnt = moe_router(x, norm_w, w_router, b_router)
    ints = info[:, 0:6].astype(jnp.int32)
    count = cnt[0, N_GROUPS:N_GROUPS + N_EXPERTS].astype(jnp.int32)
    row_token, block_expert, n_used, dest1, dest2 = _routing_plan(
        ints[:, 0], ints[:, 1], ints[:, 4], ints[:, 5], count, MOE_ROWS)
    ys = routed_experts(h, row_token, block_expert, n_used, w_gate, w_up, w_down, layer)
    return combine_slabs(ys, dest1, dest2, info, x)


def _pack_w_in(w):
    sizes = (1024, 1024, 1024, Q_LORA, KV_LORA, ROPE_B, QKV_C, 1024, H_C, H_C, N_BRANCH * D_MODEL)
    offs = [0]
    for s in sizes:
        offs.append(offs[-1] + s)
    a_q, a_k, a_v, b_dq, b_dkv, b_kr, c_qkv, c_z, c_b, c_a, g_in = (
        w[:, :, offs[i]:offs[i + 1]] for i in range(len(sizes)))
    zeros = jnp.zeros((w.shape[0], D_MODEL, 2 * LANES - ROPE_B - 2 * H_C), w.dtype)
    slab = jnp.concatenate([a_q, a_k, a_v, b_dq, b_dkv, b_kr, c_b, c_a, zeros, c_qkv, c_z], axis=2)
    return slab.astype(BF16), g_in.astype(BF16)


def _pack_mla(w_uq, w_ukv):
    wq = w_uq.reshape(Q_LORA, H_B, NOPE_B + ROPE_B)
    wq = jnp.pad(wq, ((0, 0), (0, 0), (0, LANES - NOPE_B - ROPE_B))).reshape(Q_LORA, H_B * LANES)
    wkv = w_ukv.reshape(KV_LORA, H_B, NOPE_B + V_B)
    wk = jnp.pad(wkv[:, :, :NOPE_B], ((0, 0), (0, 0), (0, LANES - NOPE_B))).reshape(KV_LORA, H_B * LANES)
    wv = wkv[:, :, NOPE_B:].reshape(KV_LORA, H_B * V_B)
    place = jnp.zeros((ROPE_B, H_B, LANES), F32)
    place = place.at[jnp.arange(ROPE_B), :, NOPE_B + jnp.arange(ROPE_B)].set(1.0)
    return wq.astype(BF16), wk.astype(BF16), place.reshape(ROPE_B, H_B * LANES).astype(BF16), wv.astype(BF16)


def _misc_row(vals, lane0):
    return jnp.zeros((1, LANES), F32).at[0, lane0:lane0 + H_C].set(vals.astype(F32))


def kernel(x_prompt, x_sample, cache_diff_k, cache_diff_v, cache_mla_ckv, cache_mla_krope, state_gdn_conv, state_gdn_s, norm_mix, w_in, diff_lambda, diff_subln, mla_q_norm, mla_w_uq, mla_kv_norm, mla_w_ukv, gdn_conv, gdn_a_log, gdn_dt_bias, gdn_norm, w_branch, w_out, norm_ffn, router_group, router_group_bias, router_expert, router_expert_bias, expert_w_gate, expert_w_up, expert_w_down, norm_final):
    bp, tp, _ = x_prompt.shape
    bs, ts, _ = x_sample.shape
    depth = w_in.shape[0]
    past = cache_diff_k.shape[2]
    n_p, n_s = bp * tp, bs * ts
    tk_s = past + ts

    x = jnp.concatenate([x_prompt.reshape(n_p, D_MODEL), x_sample.reshape(n_s, D_MODEL)], axis=0)
    pos = jnp.concatenate([jnp.tile(jnp.arange(tp), bp), jnp.tile(past + jnp.arange(ts), bs)])
    tabs = (_rope_tables(pos, DH_A, 0, ROT_A // 2)
            + _rope_tables(pos, LANES, NOPE_B, ROPE_B // 2)
            + _rope_tables(pos, LANES, 0, ROPE_B // 2))

    zero_conv = jnp.zeros((bp, CONV_W - 1, QKV_C), F32)
    zero_state = jnp.zeros((bp, H_C, DV_C, DK_C), F32)
    zero_branch = jnp.zeros((n_p + n_s, BRANCH_W), BF16)
    last = np.arange(-(CONV_W - 1), 0)
    tail_p = ((np.arange(bp)[:, None] + 1) * tp + last[None, :]).reshape(-1)
    tail_s = (n_p + (np.arange(bs)[:, None] + 1) * ts + last[None, :]).reshape(-1)
    st_p, st_s = [], []
    w_slab, w_gates = _pack_w_in(w_in)
    w_branch16 = w_branch.astype(BF16)
    w_out16 = w_out.astype(BF16)
    for l in range(depth):
        lam_init = 0.8 - 0.6 * math.exp(-0.3 * l)
        wq, wk, p_kr, wv = _pack_mla(mla_w_uq[l], mla_w_ukv[l])

        h = rmsnorm_rows(x, norm_mix[l], BF16)
        slab = matmul(h, w_slab, l, F32, name="proj_in")
        gates = matmul(h, w_gates, l, BF16, act="sigmoid", name="proj_gates")
        qa, ka, ka_bf, va, va_bf, qb, ckv, kr = mixer_prep(slab, tabs, mla_q_norm[l], wq, mla_kv_norm[l])

        diff_kw = dict(diff=True, lamp=diff_lambda[l], subln=diff_subln[l], lam_init=lam_init)
        oa = attention(qa, ka_bf, va_bf, zero_branch, n_streams=bp, q_row0=0, t_q=tp, t_k=tp, q_off=0,
                       tq=ATTN_TQ, tk=ATTN_TK, name="diff_attn_p", **diff_kw)
        oa = attention_cached(qa, ka_bf, va_bf, cache_diff_k[l].reshape(bs, past, 1024),
                              cache_diff_v[l].reshape(bs, past, 1024), oa, row0=n_p,
                              lamp=diff_lambda[l], subln=diff_subln[l], lam_init=lam_init)

        kb, vb = mla_expand(ckv, kr, wk, p_kr, wv)
        ob = attention(qb, kb, vb, zero_branch, n_streams=bp, q_row0=0, t_q=tp, t_k=tp, q_off=0,
                       tq=ATTN_TQ, tk=ATTN_TK, diff=False, name="mla_attn_p")
        ckv_all = jnp.concatenate([cache_mla_ckv[l], ckv[n_p:].reshape(bs, ts, KV_LORA)], axis=1)
        kr_all = jnp.concatenate([cache_mla_krope[l], kr[n_p:].reshape(bs, ts, ROPE_B)], axis=1)
        kb_s, vb_s = mla_expand(ckv_all.reshape(bs * tk_s, KV_LORA), kr_all.reshape(bs * tk_s, ROPE_B),
                                wk, p_kr, wv)
        ob = attention(qb, kb_s, vb_s, ob, n_streams=bs, q_row0=n_p, t_q=ts, t_k=tk_s, q_off=past,
                       tq=ts, tk=tk_s, diff=False, name="mla_attn_s")

        alog_row = _misc_row(gdn_a_log[l], MISC_A0)
        dtb_row = _misc_row(gdn_dt_bias[l], MISC_A0)
        oc, s_p = gated_delta(slab, zero_conv, zero_state, gdn_conv[l], alog_row, dtb_row, gdn_norm[l], zero_branch,
                              row_blk0=0, n_streams=bp, n_chunks=tp // CHUNK)
        oc, s_s = gated_delta(slab, state_gdn_conv[l], state_gdn_s[l], gdn_conv[l], alog_row, dtb_row,
                              gdn_norm[l], oc, row_blk0=n_p // CHUNK, n_streams=bs, n_chunks=ts // CHUNK)
        conv_p = slab[tail_p][:, COL_CQ:COL_CQ + QKV_C].reshape(bp, CONV_W - 1, QKV_C)
        conv_s = slab[tail_s][:, COL_CQ:COL_CQ + QKV_C].reshape(bs, CONV_W - 1, QKV_C)

        merged = branch_merge(oa, ob, oc, w_branch16, l, gates)
        x = matmul(merged, w_out16, l, F32, residual=x, name="proj_out")

        w_router = jnp.concatenate([router_group[l], router_expert[l],
                                    jnp.zeros((D_MODEL, LANES - N_GROUPS - N_EXPERTS), F32)], axis=1).astype(BF16)
        b_router = jnp.concatenate([router_group_bias[l], router_expert_bias[l],
                                    jnp.zeros((LANES - N_GROUPS - N_EXPERTS,), F32)]).reshape(1, LANES)
        x = hier_moe_residual(x, norm_ffn[l], w_router, b_router, expert_w_gate, expert_w_up, expert_w_down, l)

        st_p.append((ka[:n_p].reshape(bp, tp, H_A, 2, DH_A), va[:n_p].reshape(bp, tp, H_A, VA),
                     ckv[:n_p].reshape(bp, tp, KV_LORA), kr[:n_p].reshape(bp, tp, ROPE_B), conv_p, s_p))
        st_s.append((ka[n_p:].reshape(bs, ts, H_A, 2, DH_A), va[n_p:].reshape(bs, ts, H_A, VA),
                     ckv[n_p:].reshape(bs, ts, KV_LORA), kr[n_p:].reshape(bs, ts, ROPE_B), conv_s, s_s))

    y_p, y_s = rmsnorm_split(x, norm_final, n_p)
    y_prompt = y_p.reshape(bp, tp, D_MODEL)
    y_sample = y_s.reshape(bs, ts, D_MODEL)

    def stk(sts, i):
        return jnp.stack([s[i] for s in sts])

    return (y_prompt, y_sample,
            stk(st_p, 0), stk(st_p, 1), stk(st_p, 2), stk(st_p, 3), stk(st_p, 4), stk(st_p, 5),
            stk(st_s, 0), stk(st_s, 1), stk(st_s, 2), stk(st_s, 3), stk(st_s, 4), stk(st_s, 5))
```

```python
import functools
import math

import jax
import jax.numpy as jnp
import numpy as np
from jax import lax
from jax.experimental import pallas as pl
from jax.experimental.pallas import tpu as pltpu

F32 = jnp.float32
BF16 = jnp.bfloat16

D_MODEL = 2048
CHUNK = 64
CHUNK_SHIFT = 6
ROPE_THETA = 500000.0
EPS = 1e-6
N_BRANCH = 3
H_A = 8
DH_A = 64
VA = 2 * DH_A
ROT_A = DH_A // 4
H_B = 8
Q_LORA = 512
KV_LORA = 256
NOPE_B = 64
ROPE_B = 32
V_B = 128
H_C = 8
DK_C = 128
DV_C = 128
CONV_W = 4
QKV_C = 2 * H_C * DK_C + H_C * DV_C
BRANCH_W = H_A * VA
N_GROUPS = 4
EXPERTS_PER_GROUP = 4
N_EXPERTS = N_GROUPS * EXPERTS_PER_GROUP
D_EXPERT = 512

LANES = 128
VMEM_LIMIT_BYTES = 48 * 1024 * 1024
EXPERT_VMEM_LIMIT_BYTES = 56 * 1024 * 1024
NEG_BIG = -1e30
LOG2_E = math.log2(math.e)

COL_AQ, COL_AK, COL_AV = 0, 1024, 2048
COL_DQ, COL_DKV, COL_MISC = 3072, 3584, 3840
COL_CQ, COL_CK, COL_CV, COL_CZ = 4096, 5120, 6144, 7168
SLAB_F = 8192
MISC_B0 = ROPE_B
MISC_A0 = ROPE_B + H_C

MOE_ROWS = 256
MM_ROWS = 1088
SLAB_ROWS = D_MODEL // LANES
ATTN_TQ = 512
ATTN_TK = 512


def _cparams(*sem):
    return pltpu.CompilerParams(dimension_semantics=sem, vmem_limit_bytes=VMEM_LIMIT_BYTES)


def _sigmoid(x):
    return 1.0 / (1.0 + jnp.exp(-x))


def _silu(x):
    return x * _sigmoid(x)


def _rmsnorm_kernel(x_ref, w_ref, o_ref):
    x = x_ref[...]
    ms = jnp.mean(x * x, axis=-1, keepdims=True)
    o_ref[...] = (x * lax.rsqrt(ms + EPS) * w_ref[...]).astype(o_ref.dtype)


def rmsnorm_rows(x, w, out_dtype, tm=512):
    m, d = x.shape
    return pl.pallas_call(
        _rmsnorm_kernel,
        grid=(m // tm,),
        in_specs=[pl.BlockSpec((tm, d), lambda i: (i, 0)),
                  pl.BlockSpec((1, d), lambda i: (0, 0))],
        out_specs=pl.BlockSpec((tm, d), lambda i: (i, 0)),
        out_shape=jax.ShapeDtypeStruct((m, d), out_dtype),
        compiler_params=_cparams("parallel"),
        name="rmsnorm",
    )(x, w.reshape(1, d))


def _rmsnorm_split_kernel(x_ref, w_ref, o1_ref, o2_ref, *, n1):
    x = x_ref[...]
    y = x * lax.rsqrt(jnp.mean(x * x, axis=-1, keepdims=True) + EPS) * w_ref[...]
    i = pl.program_id(0)

    @pl.when(i < n1)
    def _():
        o1_ref[...] = y

    @pl.when(i >= n1)
    def _():
        o2_ref[...] = y


def rmsnorm_split(x, w, rows1, tm=512):
    m, d = x.shape
    n1 = rows1 // tm
    return pl.pallas_call(
        functools.partial(_rmsnorm_split_kernel, n1=n1),
        grid=(m // tm,),
        in_specs=[pl.BlockSpec((tm, d), lambda i: (i, 0)),
                  pl.BlockSpec((1, d), lambda i: (0, 0))],
        out_specs=[pl.BlockSpec((tm, d), lambda i: (jnp.minimum(i, n1 - 1), 0)),
                   pl.BlockSpec((tm, d), lambda i: (jnp.maximum(i - n1, 0), 0))],
        out_shape=[jax.ShapeDtypeStruct((rows1, d), F32), jax.ShapeDtypeStruct((m - rows1, d), F32)],
        compiler_params=_cparams("arbitrary"),
        name="rmsnorm_final",
    )(x, w.reshape(1, d))


def _mm_kernel(a_ref, b_ref, o_ref, *, act):
    acc = jnp.dot(a_ref[...], b_ref[...], preferred_element_type=F32)
    if act == "sigmoid":
        acc = _sigmoid(acc)
    o_ref[...] = acc.astype(o_ref.dtype)


def _mm_res_kernel(a_ref, b_ref, r_ref, o_ref):
    acc = jnp.dot(a_ref[...], b_ref[...], preferred_element_type=F32)
    o_ref[...] = r_ref[...] + acc


def matmul(a, b, layer, out_dtype, act=None, residual=None, tm=MM_ROWS, tn=1024, name="mm"):
    m, k = a.shape
    n = b.shape[2]
    in_specs = [pl.BlockSpec((tm, k), lambda j, i: (i, 0)),
                pl.BlockSpec((None, k, tn), lambda j, i: (layer, 0, j))]
    args = [a, b]
    if residual is None:
        body = functools.partial(_mm_kernel, act=act)
    else:
        body = _mm_res_kernel
        in_specs.append(pl.BlockSpec((tm, tn), lambda j, i: (i, j)))
        args.append(residual)
    return pl.pallas_call(
        body,
        grid=(n // tn, m // tm),
        in_specs=in_specs,
        out_specs=pl.BlockSpec((tm, tn), lambda j, i: (i, j)),
        out_shape=jax.ShapeDtypeStruct((m, n), out_dtype),
        compiler_params=_cparams("parallel", "parallel"),
        name=name,
    )(*args)


def _rope_tables(pos, period, offset, half):
    rot = 2 * half
    inv = ROPE_THETA ** (-jnp.arange(half, dtype=F32) * 2.0 / rot)
    ang = pos.astype(F32)[:, None] * inv[None, :]
    cos, sin = jnp.cos(ang), jnp.sin(ang)
    lane = jnp.arange(LANES)
    r = (lane % period) - offset
    first = (r >= 0) & (r < half)
    second = (r >= half) & (r < rot)
    idx = jnp.clip(jnp.where(second, r - half, r), 0, half - 1)
    cos_l, sin_l = cos[:, idx], sin[:, idx]
    c = jnp.where((first | second)[None, :], cos_l, 1.0)
    sa = jnp.where(first[None, :], -sin_l, 0.0)
    sb = jnp.where(second[None, :], sin_l, 0.0)
    return c, sa, sb


def _rope_lanes(x, c, sa, sb, half):
    return (x * c + pltpu.roll(x, LANES - half, axis=1) * sa
            + pltpu.roll(x, half, axis=1) * sb)


def _prep_kernel(aq_ref, ak_ref, av_ref, dq_ref, dkv_ref, misc_ref,
                 ca_ref, saa_ref, sba_ref, cb_ref, sab_ref, sbb_ref, ck_ref, sak_ref, sbk_ref,
                 qn_ref, wuq_ref, kvn_ref,
                 qa_o, ka_o, kabf_o, va_o, vabf_o, qb_o, ckv_o, kr_o):
    ca, saa, sba = ca_ref[...], saa_ref[...], sba_ref[...]
    for c in range(H_A):
        sl = slice(c * LANES, (c + 1) * LANES)
        q = _rope_lanes(aq_ref[:, sl], ca, saa, sba, ROT_A // 2)
        qa_o[:, sl] = (q * (DH_A ** -0.5 * LOG2_E)).astype(BF16)
        k = _rope_lanes(ak_ref[:, sl], ca, saa, sba, ROT_A // 2)
        ka_o[:, sl] = k
        kabf_o[:, sl] = k.astype(BF16)
    v = av_ref[...]
    va_o[...] = v
    vabf_o[...] = v.astype(BF16)

    x = dq_ref[...]
    cq = x * lax.rsqrt(jnp.mean(x * x, axis=-1, keepdims=True) + EPS) * qn_ref[...]
    qb = jnp.dot(cq.astype(BF16), wuq_ref[...], preferred_element_type=F32)
    cb, sab, sbb = cb_ref[...], sab_ref[...], sbb_ref[...]
    scale_b = (NOPE_B + ROPE_B) ** -0.5 * LOG2_E
    for c in range(H_B):
        sl = slice(c * LANES, (c + 1) * LANES)
        q = _rope_lanes(qb[:, sl], cb, sab, sbb, ROPE_B // 2)
        qb_o[:, sl] = (q * scale_b).astype(BF16)

    x = dkv_ref[...]
    ckv_o[...] = x * lax.rsqrt(jnp.mean(x * x, axis=-1, keepdims=True) + EPS) * kvn_ref[...]
    kr = _rope_lanes(misc_ref[...], ck_ref[...], sak_ref[...], sbk_ref[...], ROPE_B // 2)
    kr_o[...] = kr[:, :ROPE_B]


def mixer_prep(slab, tabs, q_norm, w_uq_pad, kv_norm, tm=256):
    m = slab.shape[0]

    def col(width, start):
        blk = start // width
        return pl.BlockSpec((tm, width), lambda i: (i, blk))

    def row(width):
        return pl.BlockSpec((tm, width), lambda i: (i, 0))

    def full(shape):
        return pl.BlockSpec(shape, lambda i: (0,) * len(shape))

    in_specs = [col(1024, COL_AQ), col(1024, COL_AK), col(1024, COL_AV),
                col(Q_LORA, COL_DQ), col(KV_LORA, COL_DKV), col(LANES, COL_MISC)]
    in_specs += [row(LANES)] * 9
    in_specs += [full((1, Q_LORA)), full((Q_LORA, 1024)), full((1, KV_LORA))]
    out_shape = [jax.ShapeDtypeStruct((m, 1024), BF16),
                 jax.ShapeDtypeStruct((m, 1024), F32),
                 jax.ShapeDtypeStruct((m, 1024), BF16),
                 jax.ShapeDtypeStruct((m, 1024), F32),
                 jax.ShapeDtypeStruct((m, 1024), BF16),
                 jax.ShapeDtypeStruct((m, 1024), BF16),
                 jax.ShapeDtypeStruct((m, KV_LORA), F32),
                 jax.ShapeDtypeStruct((m, ROPE_B), F32)]
    out_specs = [row(1024)] * 6 + [row(KV_LORA), row(ROPE_B)]
    return pl.pallas_call(
        _prep_kernel,
        grid=(m // tm,),
        in_specs=in_specs,
        out_specs=out_specs,
        out_shape=out_shape,
        compiler_params=_cparams("parallel"),
        name="mixer_prep",
    )(slab, slab, slab, slab, slab, slab, *tabs,
      q_norm.reshape(1, Q_LORA), w_uq_pad, kv_norm.reshape(1, KV_LORA))


def _expand_kernel(ckv_ref, kr_ref, wuk_ref, pk_ref, wuv_ref, kb_o, vb_o):
    ckv = ckv_ref[...].astype(BF16)
    kr = kr_ref[...].astype(BF16)
    kb = (jnp.dot(ckv, wuk_ref[...], preferred_element_type=F32)
          + jnp.dot(kr, pk_ref[...], preferred_element_type=F32))
    kb_o[...] = kb.astype(BF16)
    vb_o[...] = jnp.dot(ckv, wuv_ref[...], preferred_element_type=F32).astype(BF16)


def mla_expand(ckv, kr, w_uk_pad, p_kr, w_uv, tm=512):
    m = ckv.shape[0]
    return pl.pallas_call(
        _expand_kernel,
        grid=(m // tm,),
        in_specs=[pl.BlockSpec((tm, KV_LORA), lambda i: (i, 0)),
                  pl.BlockSpec((tm, ROPE_B), lambda i: (i, 0)),
                  pl.BlockSpec((KV_LORA, 1024), lambda i: (0, 0)),
                  pl.BlockSpec((ROPE_B, 1024), lambda i: (0, 0)),
                  pl.BlockSpec((KV_LORA, 1024), lambda i: (0, 0))],
        out_specs=[pl.BlockSpec((tm, 1024), lambda i: (i, 0))] * 2,
        out_shape=[jax.ShapeDtypeStruct((m, 1024), BF16)] * 2,
        compiler_params=_cparams("parallel"),
        name="mla_expand",
    )(ckv, kr, w_uk_pad, p_kr, w_uv)


def _attn_kernel(*refs, diff, tq, tk, q_off, nk, lam_init, aliased, group):
    q_ref, k_ref, v_ref = refs[:3]
    pos = 3
    if diff:
        lamp_ref, subln_ref = refs[3:5]
        pos = 5
    if aliased:
        pos += 1
    single = nk == 1
    if single:
        o_ref, bias_sc = refs[pos:pos + 2]
    else:
        o_ref, bias_sc, m_sc, l_sc, acc_sc = refs[pos:pos + 5]
    n_maps = 2 if diff else 1
    maps = [(h, mp) for h in range(H_A) for mp in range(n_maps)]
    i = pl.program_id(1)
    j = pl.program_id(2)

    def write_head(h, outs):
        if diff:
            lp = lamp_ref[...]
            lam = (jnp.exp(jnp.sum(lp[0:1] * lp[1:2], axis=-1, keepdims=True))
                   - jnp.exp(jnp.sum(lp[2:3] * lp[3:4], axis=-1, keepdims=True)) + lam_init)
            o = outs[0] - lam * outs[1]
            o = o * lax.rsqrt(jnp.mean(o * o, axis=-1, keepdims=True) + EPS) * subln_ref[...]
            o = o * (1.0 - lam_init)
        else:
            o = outs[0]
        o_ref[:, h * LANES:(h + 1) * LANES] = o.astype(o_ref.dtype)

    if not single:
        @pl.when(j == 0)
        def _():
            m_sc[...] = jnp.full(m_sc.shape, NEG_BIG, F32)
            l_sc[...] = jnp.zeros(l_sc.shape, F32)
            acc_sc[...] = jnp.zeros(acc_sc.shape, F32)

    q_chunk_max = (q_off + i * tq + tq - 1) // CHUNK

    @pl.when((j * tk) // CHUNK <= q_chunk_max)
    def _():
        q_chunk = lax.shift_right_logical(q_off + i * tq + lax.broadcasted_iota(jnp.int32, (tq, 1), 0),
                                          CHUNK_SHIFT)
        k_chunk = lax.shift_right_logical(j * tk + lax.broadcasted_iota(jnp.int32, (1, tk), 1), CHUNK_SHIFT)
        bias_sc[...] = jnp.where(q_chunk >= k_chunk, 0.0, NEG_BIG)
        lane = lax.broadcasted_iota(jnp.int32, (1, LANES), 1)

        def scores(h, mp):
            hs = slice(h * LANES, (h + 1) * LANES)
            q = q_ref[:, hs]
            if diff:
                in_map = (lane < DH_A) if mp == 0 else (lane >= DH_A)
                q = jnp.where(in_map, q, jnp.zeros_like(q))
            return lax.dot_general(q, k_ref[:, hs], (((1,), (1,)), ((), ())),
                                   preferred_element_type=F32) + bias_sc[...]

        def pv(p, h):
            return jnp.dot(p.astype(BF16), v_ref[:, h * LANES:(h + 1) * LANES], preferred_element_type=F32)

        for g0 in range(0, len(maps), group):
            grp = maps[g0:g0 + group]
            s = [scores(h, mp) for h, mp in grp]
            m_cur = [jnp.max(x, axis=-1, keepdims=True) for x in s]
            if single:
                p = [jnp.exp2(x - m) for x, m in zip(s, m_cur)]
                l = [jnp.sum(x, axis=-1, keepdims=True) for x in p]
                outs = [pv(x, h) / d for x, d, (h, _) in zip(p, l, grp)]
                for t in range(0, len(grp), n_maps):
                    write_head(grp[t][0], outs[t:t + n_maps])
            else:
                ids = [h * n_maps + mp for h, mp in grp]
                m_prev = [m_sc[t] for t in ids]
                m_new = [jnp.maximum(a, b) for a, b in zip(m_prev, m_cur)]
                p = [jnp.exp2(x - jnp.concatenate([m] * (tk // LANES), axis=1)) for x, m in zip(s, m_new)]
                alpha = [jnp.exp2(a - b) for a, b in zip(m_prev, m_new)]
                for t, a, x, m, (h, _) in zip(ids, alpha, p, m_new, grp):
                    l_sc[t] = a * l_sc[t] + jnp.sum(x, axis=-1, keepdims=True)
                    acc_sc[t] = a * acc_sc[t] + pv(x, h)
                    m_sc[t] = m

    if not single:
        @pl.when(j == nk - 1)
        def _():
            for h in range(H_A):
                write_head(h, [acc_sc[h * n_maps + mp] / l_sc[h * n_maps + mp] for mp in range(n_maps)])


def attention(q, k, v, out_prev, *, n_streams, q_row0, t_q, t_k, q_off, tq, tk, diff,
              lamp=None, subln=None, lam_init=0.0, group=2, name="attn"):
    nq, nk = t_q // tq, t_k // tk
    qb0 = q_row0 // tq
    width = H_A * LANES

    def q_map(b, i, j):
        return (qb0 + b * nq + i, 0)

    def kv_map(b, i, j):
        last = ((q_off + i * tq + tq - 1) // CHUNK * CHUNK) // tk
        return (b * nk + jnp.minimum(j, last), 0)

    in_specs = [pl.BlockSpec((tq, width), q_map),
                pl.BlockSpec((tk, width), kv_map),
                pl.BlockSpec((tk, width), kv_map)]
    args = [q, k, v]
    n_maps = 1
    if diff:
        n_maps = 2
        in_specs += [pl.BlockSpec((4, DH_A), lambda b, i, j: (0, 0)),
                     pl.BlockSpec((1, VA), lambda b, i, j: (0, 0))]
        args += [lamp, subln.reshape(1, VA)]
    aliases = {}
    if out_prev is not None:
        aliases = {len(args): 0}
        in_specs.append(pl.BlockSpec(memory_space=pl.ANY))
        args.append(out_prev)
    return pl.pallas_call(
        functools.partial(_attn_kernel, diff=diff, tq=tq, tk=tk, q_off=q_off, nk=nk, lam_init=lam_init,
                          aliased=out_prev is not None, group=group),
        grid=(n_streams, nq, nk),
        in_specs=in_specs,
        out_specs=pl.BlockSpec((tq, width), q_map),
        out_shape=jax.ShapeDtypeStruct((q.shape[0], width), BF16),
        scratch_shapes=[pltpu.VMEM((tq, tk), F32)] + ([] if nk == 1 else
                                                      [pltpu.VMEM((H_A * n_maps, tq, LANES), F32)] * 3),
        input_output_aliases=aliases,
        compiler_params=_cparams("parallel", "parallel", "arbitrary"),
        name=name,
    )(*args)


def _attn_cached_kernel(q_ref, kn_ref, vn_ref, kc_ref, vc_ref, lamp_ref, subln_ref, prev_ref, o_ref, bias_sc,
                        *, ts, past, lam_init):
    del prev_ref
    pos = past + lax.broadcasted_iota(jnp.int32, (ts, 1), 0)
    kpos = past + lax.broadcasted_iota(jnp.int32, (1, ts), 1)
    bias_sc[...] = jnp.where(lax.shift_right_logical(pos, CHUNK_SHIFT)
                             >= lax.shift_right_logical(kpos, CHUNK_SHIFT), 0.0, NEG_BIG)
    lane = lax.broadcasted_iota(jnp.int32, (1, LANES), 1)
    lp = lamp_ref[...]
    lam = (jnp.exp(jnp.sum(lp[0:1] * lp[1:2], axis=-1, keepdims=True))
           - jnp.exp(jnp.sum(lp[2:3] * lp[3:4], axis=-1, keepdims=True)) + lam_init)
    nt = (((1,), (1,)), ((), ()))
    for h in range(H_A):
        hs = slice(h * LANES, (h + 1) * LANES)
        q = q_ref[:, hs]
        qs = [jnp.where(lane < DH_A, q, jnp.zeros_like(q)), jnp.where(lane >= DH_A, q, jnp.zeros_like(q))]
        kc = kc_ref[0, :, hs].astype(BF16)
        vc = vc_ref[0, :, hs].astype(BF16)
        kn = kn_ref[:, hs]
        vn = vn_ref[:, hs]
        s_old = [lax.dot_general(x, kc, nt, preferred_element_type=F32) for x in qs]
        s_new = [lax.dot_general(x, kn, nt, preferred_element_type=F32) + bias_sc[...] for x in qs]
        m = [jnp.maximum(jnp.max(a, axis=-1, keepdims=True), jnp.max(b, axis=-1, keepdims=True))
             for a, b in zip(s_old, s_new)]
        p_old = [jnp.exp2(a - t) for a, t in zip(s_old, m)]
        p_new = [jnp.exp2(b - t) for b, t in zip(s_new, m)]
        l = [jnp.sum(a, axis=-1, keepdims=True) + jnp.sum(b, axis=-1, keepdims=True)
             for a, b in zip(p_old, p_new)]
        outs = [(jnp.dot(a.astype(BF16), vc, preferred_element_type=F32)
                 + jnp.dot(b.astype(BF16), vn, preferred_element_type=F32)) / d
                for a, b, d in zip(p_old, p_new, l)]
        o = outs[0] - lam * outs[1]
        o = o * lax.rsqrt(jnp.mean(o * o, axis=-1, keepdims=True) + EPS) * subln_ref[...]
        o_ref[:, hs] = (o * (1.0 - lam_init)).astype(o_ref.dtype)


def attention_cached(q, k_new, v_new, cache_k, cache_v, out_prev, *, row0, lamp, subln, lam_init):
    bs, past, width = cache_k.shape
    ts = (q.shape[0] - row0) // bs
    blk0 = row0 // ts
    new_spec = pl.BlockSpec((ts, width), lambda b: (blk0 + b, 0))
    cache_spec = pl.BlockSpec((1, past, width), lambda b: (b, 0, 0))
    return pl.pallas_call(
        functools.partial(_attn_cached_kernel, ts=ts, past=past, lam_init=lam_init),
        grid=(bs,),
        in_specs=[new_spec, new_spec, new_spec, cache_spec, cache_spec,
                  pl.BlockSpec((4, DH_A), lambda b: (0, 0)),
                  pl.BlockSpec((1, VA), lambda b: (0, 0)),
                  pl.BlockSpec(memory_space=pl.ANY)],
        out_specs=new_spec,
        out_shape=jax.ShapeDtypeStruct(out_prev.shape, BF16),
        scratch_shapes=[pltpu.VMEM((ts, ts), F32)],
        input_output_aliases={7: 0},
        compiler_params=_cparams("parallel"),
        name="diff_attn_s",
    )(q, k_new, v_new, cache_k, cache_v, lamp, subln.reshape(1, VA), out_prev)


def _mm_nt(a, b):
    return lax.dot_general(a, b, (((1,), (1,)), ((), ())), preferred_element_type=F32)


def _mm_nn(a, b):
    return jnp.dot(a, b, preferred_element_type=F32)


def _mm_tn(a, b):
    return lax.dot_general(a, b, (((0,), (0,)), ((), ())), preferred_element_type=F32)


def _bf(xs):
    return [x.astype(BF16) for x in xs]


def _shift_rows(x, tail, s):
    xs = pltpu.roll(x, s, axis=0)
    ts = pltpu.roll(tail, s, axis=0)
    row = lax.broadcasted_iota(jnp.int32, (8, 1), 0)
    top = jnp.where(row < s, ts, xs[0:8])
    return jnp.concatenate([top, xs[8:]], axis=0)


def _gdn_kernel(*refs, n_chunks, aliased):
    (cq_ref, ck_ref, cv_ref, cz_ref, misc_ref, convp_ref, s0_ref,
     kern_ref, alog_ref, dtb_ref, norm_ref) = refs[:11]
    o_ref, s_o, s_sc, tail_sc = refs[(12 if aliased else 11):]
    n = pl.program_id(1)
    L = CHUNK
    heads = range(H_C)

    @pl.when(n == 0)
    def _():
        s_sc[...] = s0_ref[0]
        tail_sc[...] = jnp.zeros(tail_sc.shape, F32)
        tail_sc[8 - (CONV_W - 1):8, :] = convp_ref[0]

    conv = []
    for part, ref in enumerate((cq_ref, ck_ref, cv_ref)):
        cs = slice(part * 1024, (part + 1) * 1024)
        x = ref[...]
        tail = tail_sc[:, cs]
        acc = x * kern_ref[CONV_W - 1:CONV_W, cs]
        for s in range(1, CONV_W):
            acc = acc + _shift_rows(x, tail, s) * kern_ref[CONV_W - 1 - s:CONV_W - s, cs]
        conv.append(_silu(acc))
        tail_sc[:, cs] = x[L - 8:L]
    qc, kc, vc = conv

    misc = misc_ref[...]
    beta_all = _sigmoid(misc)
    za = misc + dtb_ref[...]
    softplus = jnp.maximum(za, 0.0) + jnp.log(1.0 + jnp.exp(-jnp.abs(za)))
    g_all = -jnp.exp(alog_ref[...]) * softplus
    row = lax.broadcasted_iota(jnp.int32, (L, 1), 0)
    gcum_all = g_all
    for s in (1, 2, 4, 8, 16, 32):
        gcum_all = gcum_all + jnp.where(row >= s, pltpu.roll(gcum_all, s, axis=0), 0.0)
    lane = lax.broadcasted_iota(jnp.int32, (1, LANES), 1)
    is_beta = (lane >= MISC_B0) & (lane < MISC_B0 + H_C)
    packed = jnp.where(is_beta, beta_all, gcum_all)
    packed_t = jnp.concatenate([packed, jnp.zeros((LANES - L, LANES), F32)], axis=0).T

    ri = lax.broadcasted_iota(jnp.int32, (L, L), 0)
    ci = lax.broadcasted_iota(jnp.int32, (L, L), 1)
    incl = ri >= ci
    strict = ri > ci
    diag16 = (ri // 16) == (ci // 16)

    qn, kn, v_h, beta, gcum, decay, gam = [], [], [], [], [], [], []
    for h in heads:
        hs = slice(h * LANES, (h + 1) * LANES)
        qh, kh = qc[:, hs], kc[:, hs]
        qn.append(qh * lax.rsqrt(jnp.sum(qh * qh, axis=-1, keepdims=True) + EPS) * (DK_C ** -0.5))
        kn.append(kh * lax.rsqrt(jnp.sum(kh * kh, axis=-1, keepdims=True) + EPS))
        v_h.append(vc[:, hs])
        beta.append(beta_all[:, MISC_B0 + h:MISC_B0 + h + 1])
        gc = gcum_all[:, MISC_A0 + h:MISC_A0 + h + 1]
        gcum.append(gc)
        seg = gc - packed_t[MISC_A0 + h:MISC_A0 + h + 1, 0:L]
        decay.append(jnp.where(incl, jnp.exp(jnp.where(incl, seg, 0.0)), 0.0))
        gam.append(jnp.exp(gc))
    q16, k16 = _bf(qn), _bf(kn)
    kk = [_mm_nt(k16[h], k16[h]) for h in heads]
    pq = [_mm_nt(q16[h], k16[h]) * decay[h] for h in heads]
    a_mat = [jnp.where(strict, beta[h] * kk[h] * decay[h], 0.0) for h in heads]

    a_d = [jnp.where(diag16, a_mat[h], 0.0) for h in heads]
    a_lo = [a_mat[h] - a_d[h] for h in heads]
    ad16 = _bf(a_d)
    a2 = [_mm_nn(ad16[h], ad16[h]) for h in heads]
    a2_16 = _bf(a2)
    a4 = [_mm_nn(a2_16[h], a2_16[h]) for h in heads]
    a4_16 = _bf(a4)
    a8 = [_mm_nn(a4_16[h], a4_16[h]) for h in heads]
    f12 = [a2[h] - a_d[h] - _mm_nn(ad16[h], a2_16[h]) for h in heads]
    a8_16 = _bf(a8)
    f34 = [a4[h] + a8[h] + _mm_nn(a4_16[h], a8_16[h]) for h in heads]
    f12_16, f34_16 = _bf(f12), _bf(f34)
    g = [f12[h] + f34[h] + _mm_nn(f12_16[h], f34_16[h]) for h in heads]
    g16, alo16 = _bf(g), _bf(a_lo)
    nm = [a_lo[h] + _mm_nn(g16[h], alo16[h]) for h in heads]
    n16 = _bf(nm)
    n2 = [_mm_nn(n16[h], n16[h]) for h in heads]
    n2_16 = _bf(n2)
    hm = [n2[h] - nm[h] - _mm_nn(n16[h], n2_16[h]) for h in heads]
    hm16 = _bf(hm)
    tm1 = [hm[h] + g[h] + _mm_nn(hm16[h], g16[h]) for h in heads]
    tm1_16 = _bf(tm1)
    rhs = [jnp.concatenate([v_h[h] * beta[h], kn[h] * (beta[h] * gam[h])], axis=-1) for h in heads]
    rhs16 = _bf(rhs)
    sol = [rhs[h] + _mm_nn(tm1_16[h], rhs16[h]) for h in heads]

    s_prev = [s_sc[h] for h in heads]
    s16 = _bf(s_prev)
    w16 = _bf([sol[h][:, DV_C:] for h in heads])
    u = [sol[h][:, :DV_C] - _mm_nt(w16[h], s16[h]) for h in heads]
    u16 = _bf(u)
    pq16 = _bf(pq)
    o = [gam[h] * _mm_nt(q16[h], s16[h]) + _mm_nn(pq16[h], u16[h]) for h in heads]
    for h in heads:
        gcum_end = gcum[h][L - 1:L, :]
        ut = (u[h] * jnp.exp(gcum_end - gcum[h])).astype(BF16)
        s_sc[h] = jnp.exp(gcum_end) * s_prev[h] + _mm_tn(ut, k16[h])
    for h in heads:
        hs = slice(h * LANES, (h + 1) * LANES)
        oh = o[h] * lax.rsqrt(jnp.mean(o[h] * o[h], axis=-1, keepdims=True) + EPS) * norm_ref[...]
        o_ref[:, hs] = (oh * _silu(cz_ref[:, hs])).astype(o_ref.dtype)

    @pl.when(n == n_chunks - 1)
    def _():
        s_o[0] = s_sc[...]


def gated_delta(slab, conv_prev, s0, kern, alog_row, dtb_row, norm_w, out_prev, *, row_blk0, n_streams, n_chunks):
    def rows(b, n):
        return row_blk0 + b * n_chunks + n

    def col(start):
        blk = start // 1024
        return pl.BlockSpec((CHUNK, 1024), lambda b, n: (rows(b, n), blk))

    def full(shape):
        return pl.BlockSpec(shape, lambda b, n: (0,) * len(shape))

    in_specs = [col(COL_CQ), col(COL_CK), col(COL_CV), col(COL_CZ),
                pl.BlockSpec((CHUNK, LANES), lambda b, n: (rows(b, n), COL_MISC // LANES)),
                pl.BlockSpec((1, CONV_W - 1, QKV_C), lambda b, n: (b, 0, 0)),
                pl.BlockSpec((1, H_C, DV_C, DK_C), lambda b, n: (b, 0, 0, 0)),
                full((CONV_W, QKV_C)), full((1, LANES)), full((1, LANES)), full((1, DV_C))]
    args = [slab, slab, slab, slab, slab, conv_prev, s0, kern, alog_row, dtb_row, norm_w.reshape(1, DV_C)]
    aliases = {}
    if out_prev is not None:
        aliases = {len(args): 0}
        in_specs.append(pl.BlockSpec(memory_space=pl.ANY))
        args.append(out_prev)
    return pl.pallas_call(
        functools.partial(_gdn_kernel, n_chunks=n_chunks, aliased=out_prev is not None),
        grid=(n_streams, n_chunks),
        in_specs=in_specs,
        out_specs=[pl.BlockSpec((CHUNK, 1024), lambda b, n: (rows(b, n), 0)),
                   pl.BlockSpec((1, H_C, DV_C, DK_C), lambda b, n: (b, 0, 0, 0))],
        out_shape=[jax.ShapeDtypeStruct((slab.shape[0], 1024), BF16),
                   jax.ShapeDtypeStruct((n_streams, H_C, DV_C, DK_C), F32)],
        scratch_shapes=[pltpu.VMEM((H_C, DV_C, DK_C), F32),
                        pltpu.VMEM((8, QKV_C), F32)],
        input_output_aliases=aliases,
        compiler_params=_cparams("parallel", "arbitrary"),
        name="gated_delta",
    )(*args)


def _merge_kernel(oa_ref, ob_ref, oc_ref, wb_ref, ga_ref, gb_ref, gc_ref, o_ref):
    acc = jnp.dot(oa_ref[...], wb_ref[0], preferred_element_type=F32) * ga_ref[...].astype(F32)
    acc = acc + jnp.dot(ob_ref[...], wb_ref[1], preferred_element_type=F32) * gb_ref[...].astype(F32)
    acc = acc + jnp.dot(oc_ref[...], wb_ref[2], preferred_element_type=F32) * gc_ref[...].astype(F32)
    o_ref[...] = acc.astype(o_ref.dtype)


def branch_merge(oa, ob, oc, w_branch, layer, gates, tm=MM_ROWS // 2, tn=1024):
    m = oa.shape[0]
    nj = D_MODEL // tn

    def gate_spec(nb):
        return pl.BlockSpec((tm, tn), lambda j, i: (i, nb * nj + j))

    return pl.pallas_call(
        _merge_kernel,
        grid=(nj, m // tm),
        in_specs=[pl.BlockSpec((tm, BRANCH_W), lambda j, i: (i, 0))] * 3
                 + [pl.BlockSpec((None, N_BRANCH, BRANCH_W, tn), lambda j, i: (layer, 0, 0, j))]
                 + [gate_spec(0), gate_spec(1), gate_spec(2)],
        out_specs=pl.BlockSpec((tm, tn), lambda j, i: (i, j)),
        out_shape=jax.ShapeDtypeStruct((m, D_MODEL), BF16),
        compiler_params=_cparams("parallel", "parallel"),
        name="branch_merge",
    )(oa, ob, oc, w_branch, gates, gates, gates)


def _slab_store(ref, x, rows):
    for c in range(SLAB_ROWS):
        ref[pl.ds(c, rows, stride=SLAB_ROWS), :] = x[:, c * LANES:(c + 1) * LANES]


def _slab_load(ref, rows):
    return jnp.concatenate([ref[pl.ds(c, rows, stride=SLAB_ROWS), :] for c in range(SLAB_ROWS)], axis=1)


def _router_kernel(x_ref, nw_ref, wr_ref, br_ref, h_o, info_o, cnt_o, cnt_sc, tri_sc, *, tm):
    step = pl.program_id(0)

    @pl.when(step == 0)
    def _():
        cnt_sc[...] = jnp.zeros(cnt_sc.shape, F32)
        ri = lax.broadcasted_iota(jnp.int32, (tm, tm), 0)
        ci = lax.broadcasted_iota(jnp.int32, (tm, tm), 1)
        tri_sc[...] = (ri > ci).astype(BF16)

    x = x_ref[...]
    h = (x * lax.rsqrt(jnp.mean(x * x, axis=-1, keepdims=True) + EPS) * nw_ref[...])
    _slab_store(h_o, h, tm)
    logits = jnp.dot(h.astype(BF16), wr_ref[...], preferred_element_type=F32) + br_ref[...]
    lane = lax.broadcasted_iota(jnp.int32, logits.shape, 1)
    is_g = lane < N_GROUPS
    lg = jnp.where(is_g, logits, NEG_BIG)
    eg = jnp.where(is_g, jnp.exp(lg - jnp.max(lg, axis=-1, keepdims=True)), 0.0)
    gp = eg / jnp.sum(eg, axis=-1, keepdims=True)
    pg = jnp.max(gp, axis=-1, keepdims=True)
    grp = jnp.min(jnp.where(is_g & (gp == pg), lane, LANES), axis=-1, keepdims=True)
    lo = N_GROUPS + grp * EXPERTS_PER_GROUP
    is_e = (lane >= lo) & (lane < lo + EXPERTS_PER_GROUP)
    le = jnp.where(is_e, logits, NEG_BIG)
    ee = jnp.where(is_e, jnp.exp(le - jnp.max(le, axis=-1, keepdims=True)), 0.0)
    ep = ee / jnp.sum(ee, axis=-1, keepdims=True)
    v1 = jnp.max(jnp.where(is_e, ep, -1.0), axis=-1, keepdims=True)
    i1 = jnp.min(jnp.where(is_e & (ep == v1), lane, LANES), axis=-1, keepdims=True)
    rest = is_e & (lane != i1)
    v2 = jnp.max(jnp.where(rest, ep, -1.0), axis=-1, keepdims=True)
    i2 = jnp.min(jnp.where(rest & (ep == v2), lane, LANES), axis=-1, keepdims=True)
    den = v1 + v2
    w1 = pg * v1 / den
    w2 = pg * v2 / den

    oh1 = lane == i1
    oh2 = lane == i2
    picked = (oh1 | oh2).astype(F32)
    before = jnp.dot(tri_sc[...], picked.astype(BF16), preferred_element_type=F32) + cnt_sc[...]
    rank1 = jnp.sum(jnp.where(oh1, before, 0.0), axis=-1, keepdims=True)
    rank2 = jnp.sum(jnp.where(oh2, before, 0.0), axis=-1, keepdims=True)
    cnt = cnt_sc[...] + jnp.sum(picked, axis=0, keepdims=True)
    cnt_sc[...] = cnt
    cnt_o[...] = cnt

    info = jnp.where(lane == 0, (i1 - N_GROUPS).astype(F32), 0.0)
    info = jnp.where(lane == 1, (i2 - N_GROUPS).astype(F32), info)
    info = jnp.where(lane == 2, w1, info)
    info = jnp.where(lane == 3, w2, info)
    info = jnp.where(lane == 4, rank1, info)
    info = jnp.where(lane == 5, rank2, info)
    info_o[...] = info


def moe_router(x, norm_w, w_router, b_router, tm=512):
    m = x.shape[0]
    return pl.pallas_call(
        functools.partial(_router_kernel, tm=tm),
        grid=(m // tm,),
        in_specs=[pl.BlockSpec((tm, D_MODEL), lambda i: (i, 0)),
                  pl.BlockSpec((1, D_MODEL), lambda i: (0, 0)),
                  pl.BlockSpec((D_MODEL, LANES), lambda i: (0, 0)),
                  pl.BlockSpec((1, LANES), lambda i: (0, 0))],
        out_specs=[pl.BlockSpec((tm * SLAB_ROWS, LANES), lambda i: (i, 0)),
                   pl.BlockSpec((tm, LANES), lambda i: (i, 0)),
                   pl.BlockSpec((1, LANES), lambda i: (0, 0))],
        out_shape=[jax.ShapeDtypeStruct((m * SLAB_ROWS, LANES), F32),
                   jax.ShapeDtypeStruct((m, LANES), F32),
                   jax.ShapeDtypeStruct((1, LANES), F32)],
        scratch_shapes=[pltpu.VMEM((1, LANES), F32), pltpu.VMEM((tm, tm), BF16)],
        compiler_params=_cparams("arbitrary"),
        name="moe_router",
    )(x, norm_w.reshape(1, D_MODEL), w_router, b_router)


def _slab_copy(src_ref, dst_ref, sem, tok, r):
    return pltpu.make_async_copy(src_ref.at[pl.ds(tok * SLAB_ROWS, SLAB_ROWS), :],
                                 dst_ref.at[pl.ds(r * SLAB_ROWS, SLAB_ROWS), :], sem)


def _gather_start(src_ref, dst_ref, sem, idx_ref, rows):
    def body(r2, c):
        for u in range(2):
            r = r2 * 2 + u
            _slab_copy(src_ref, dst_ref, sem, idx_ref[0, 0, r], r).start(priority=u)
        return c

    lax.fori_loop(0, rows // 2, body, 0, unroll=4)


def _gather_wait(src_ref, dst_ref, sem, rows):
    def body(r, c):
        _slab_copy(src_ref, dst_ref, sem, 0, r).wait()
        return c

    lax.fori_loop(0, rows, body, 0, unroll=8)


def _combine_kernel(i1c_ref, i2c_ref, i1n_ref, i2n_ref, src_ref, info_ref, res_ref, o_ref, buf1, buf2, sem,
                    *, rows):
    step = pl.program_id(0)
    slot = lax.rem(step, 2)

    def start(i1_ref, i2_ref, s):
        _gather_start(src_ref, buf1.at[s], sem.at[0, s], i1_ref, rows)
        _gather_start(src_ref, buf2.at[s], sem.at[1, s], i2_ref, rows)

    @pl.when(step == 0)
    def _():
        start(i1c_ref, i2c_ref, 0)

    @pl.when(step + 1 < pl.num_programs(0))
    def _():
        start(i1n_ref, i2n_ref, 1 - slot)

    _gather_wait(src_ref, buf1.at[slot], sem.at[0, slot], rows)
    _gather_wait(src_ref, buf2.at[slot], sem.at[1, slot], rows)
    info = info_ref[...]
    o_ref[...] = (res_ref[...] + info[:, 2:3] * _slab_load(buf1.at[slot], rows)
                  + info[:, 3:4] * _slab_load(buf2.at[slot], rows))


def combine_slabs(ys, dest1, dest2, info, residual, rows=MOE_ROWS):
    n = dest1.shape[0]
    nb = n // rows
    cur = pl.BlockSpec((1, 1, rows), lambda i: (i, 0, 0), memory_space=pltpu.SMEM)
    nxt = pl.BlockSpec((1, 1, rows), lambda i: (jnp.minimum(i + 1, nb - 1), 0, 0), memory_space=pltpu.SMEM)
    d1, d2 = dest1.reshape(nb, 1, rows), dest2.reshape(nb, 1, rows)
    return pl.pallas_call(
        functools.partial(_combine_kernel, rows=rows),
        grid=(nb,),
        in_specs=[cur, cur, nxt, nxt, pl.BlockSpec(memory_space=pl.ANY),
                  pl.BlockSpec((rows, LANES), lambda i: (i, 0)),
                  pl.BlockSpec((rows, D_MODEL), lambda i: (i, 0))],
        out_specs=pl.BlockSpec((rows, D_MODEL), lambda i: (i, 0)),
        out_shape=jax.ShapeDtypeStruct((n, D_MODEL), F32),
        scratch_shapes=[pltpu.VMEM((2, rows * SLAB_ROWS, LANES), F32)] * 2 + [pltpu.SemaphoreType.DMA((2, 2))],
        compiler_params=_cparams("arbitrary"),
        name="moe_combine",
    )(d1, d2, d1, d2, ys, info, residual)


def _expert_kernel(be_ref, nu_ref, idx0_ref, idxn_ref, h_ref, wg_ref, wu_ref, wd_ref, o_ref,
                   xbuf, sem, wg_sc, wu_sc, wd_sc, *, rows):
    b = pl.program_id(0)
    n_used = nu_ref[0]
    slot = lax.rem(b, 2)
    prev = be_ref[jnp.maximum(b - 1, 0)]

    @pl.when(b == 0)
    def _():
        _gather_start(h_ref, xbuf.at[0], sem.at[0], idx0_ref, rows)

    @pl.when(b + 1 < n_used)
    def _():
        _gather_start(h_ref, xbuf.at[1 - slot], sem.at[1 - slot], idxn_ref, rows)

    @pl.when((b == 0) | (be_ref[b] != prev))
    def _():
        wg_sc[...] = wg_ref[...].astype(BF16)
        wu_sc[...] = wu_ref[...].astype(BF16)
        wd_sc[...] = wd_ref[...].astype(BF16)

    @pl.when(b < n_used)
    def _():
        _gather_wait(h_ref, xbuf.at[slot], sem.at[slot], rows)
        x = _slab_load(xbuf.at[slot], rows).astype(BF16)
        g = jnp.dot(x, wg_sc[...], preferred_element_type=F32)
        u = jnp.dot(x, wu_sc[...], preferred_element_type=F32)
        hid = _silu(g) * u
        _slab_store(o_ref, jnp.dot(hid.astype(BF16), wd_sc[...], preferred_element_type=F32), rows)

    @pl.when(b >= n_used)
    def _():
        o_ref[...] = jnp.zeros(o_ref.shape, F32)


def routed_experts(h, row_token, block_expert, n_used, w_gate, w_up, w_down, layer, rows=MOE_ROWS):
    n_slots = row_token.shape[0]
    nb = n_slots // rows
    grid_spec = pltpu.PrefetchScalarGridSpec(
        num_scalar_prefetch=2,
        grid=(nb,),
        in_specs=[pl.BlockSpec((1, 1, rows), lambda b, be, nu: (0, 0, 0), memory_space=pltpu.SMEM),
                  pl.BlockSpec((1, 1, rows), lambda b, be, nu: (jnp.minimum(b + 1, nb - 1), 0, 0),
                               memory_space=pltpu.SMEM),
                  pl.BlockSpec(memory_space=pl.ANY),
                  pl.BlockSpec((None, None, D_MODEL, D_EXPERT), lambda b, be, nu: (layer, be[b], 0, 0)),
                  pl.BlockSpec((None, None, D_MODEL, D_EXPERT), lambda b, be, nu: (layer, be[b], 0, 0)),
                  pl.BlockSpec((None, None, D_EXPERT, D_MODEL), lambda b, be, nu: (layer, be[b], 0, 0))],
        out_specs=pl.BlockSpec((rows * SLAB_ROWS, LANES), lambda b, be, nu: (b, 0)),
        scratch_shapes=[pltpu.VMEM((2, rows * SLAB_ROWS, LANES), F32),
                        pltpu.SemaphoreType.DMA((2,)),
                        pltpu.VMEM((D_MODEL, D_EXPERT), BF16),
                        pltpu.VMEM((D_MODEL, D_EXPERT), BF16),
                        pltpu.VMEM((D_EXPERT, D_MODEL), BF16)],
    )
    idx = row_token.reshape(nb, 1, rows)
    return pl.pallas_call(
        functools.partial(_expert_kernel, rows=rows),
        grid_spec=grid_spec,
        out_shape=jax.ShapeDtypeStruct((n_slots * SLAB_ROWS, LANES), F32),
        compiler_params=pltpu.CompilerParams(dimension_semantics=("arbitrary",),
                                             vmem_limit_bytes=EXPERT_VMEM_LIMIT_BYTES),
        name="routed_experts",
    )(block_expert, n_used, idx, idx, h, w_gate, w_up, w_down)


def _routing_plan(e1, e2, rank1, rank2, count, rows):
    n = e1.shape[0]
    n_slots = 2 * n + N_EXPERTS * rows
    padded = (count + rows - 1) // rows * rows
    ends = jnp.cumsum(padded)
    starts = ends - padded
    dest1 = starts[e1] + rank1
    dest2 = starts[e2] + rank2
    tok = jnp.arange(n, dtype=jnp.int32)
    row_token = jnp.zeros((n_slots,), jnp.int32).at[jnp.concatenate([dest1, dest2])].set(
        jnp.concatenate([tok, tok]), unique_indices=True)
    blk_start = jnp.arange(n_slots // rows, dtype=jnp.int32) * rows
    block_expert = jnp.minimum(jnp.sum(ends[None, :] <= blk_start[:, None], axis=1), N_EXPERTS - 1).astype(jnp.int32)
    n_used = (ends[-1] // rows).astype(jnp.int32).reshape(1)
    return row_token, block_expert, n_used, dest1, dest2


def hier_moe_residual(x, norm_w, w_router, b_router, w_gate, w_up, w_down, layer):
    h, info, cnt = moe_router(x, norm_w, w_router, b_router)
    ints = info[:, 0:6].astype(jnp.int32)
    count = cnt[0, N_GROUPS:N_GROUPS + N_EXPERTS].astype(jnp.int32)
    row_token, block_expert, n_used, dest1, dest2 = _routing_plan(
        ints[:, 0], ints[:, 1], ints[:, 4], ints[:, 5], count, MOE_ROWS)
    ys = routed_experts(h, row_token, block_expert, n_used, w_gate, w_up, w_down, layer)
    return combine_slabs(ys, dest1, dest2, info, x)


def _pack_w_in_kernel(w_ref, slab_ref, gates_ref):
    w = w_ref[...]
    o_kr = COL_MISC
    o_cqkv = o_kr + ROPE_B
    o_cb = o_cqkv + QKV_C + H_C * DV_C
    o_g = o_cb + 2 * H_C
    pad = jnp.zeros((w.shape[0], 2 * LANES - ROPE_B - 2 * H_C), F32)
    slab = jnp.concatenate([w[:, :o_cqkv], w[:, o_cb:o_g], pad, w[:, o_cqkv:o_cb]], axis=1)
    slab_ref[...] = slab.astype(BF16)
    gates_ref[...] = w[:, o_g:].astype(BF16)


def _pack_w_in(w, tr=128):
    depth, d, cols = w.shape
    return pl.pallas_call(
        _pack_w_in_kernel,
        grid=(depth, d // tr),
        in_specs=[pl.BlockSpec((None, tr, cols), lambda l, i: (l, i, 0))],
        out_specs=[pl.BlockSpec((None, tr, SLAB_F), lambda l, i: (l, i, 0)),
                   pl.BlockSpec((None, tr, N_BRANCH * D_MODEL), lambda l, i: (l, i, 0))],
        out_shape=[jax.ShapeDtypeStruct((depth, d, SLAB_F), BF16),
                   jax.ShapeDtypeStruct((depth, d, N_BRANCH * D_MODEL), BF16)],
        compiler_params=_cparams("parallel", "parallel"),
        name="pack_w_in",
    )(w)


def _pack_mla(w_uq, w_ukv):
    wq = w_uq.reshape(Q_LORA, H_B, NOPE_B + ROPE_B)
    wq = jnp.pad(wq, ((0, 0), (0, 0), (0, LANES - NOPE_B - ROPE_B))).reshape(Q_LORA, H_B * LANES)
    wkv = w_ukv.reshape(KV_LORA, H_B, NOPE_B + V_B)
    wk = jnp.pad(wkv[:, :, :NOPE_B], ((0, 0), (0, 0), (0, LANES - NOPE_B))).reshape(KV_LORA, H_B * LANES)
    wv = wkv[:, :, NOPE_B:].reshape(KV_LORA, H_B * V_B)
    place = jnp.zeros((ROPE_B, H_B, LANES), F32)
    place = place.at[jnp.arange(ROPE_B), :, NOPE_B + jnp.arange(ROPE_B)].set(1.0)
    return wq.astype(BF16), wk.astype(BF16), place.reshape(ROPE_B, H_B * LANES).astype(BF16), wv.astype(BF16)


def _misc_row(vals, lane0):
    return jnp.zeros((1, LANES), F32).at[0, lane0:lane0 + H_C].set(vals.astype(F32))


def kernel(x_prompt, x_sample, cache_diff_k, cache_diff_v, cache_mla_ckv, cache_mla_krope, state_gdn_conv, state_gdn_s, norm_mix, w_in, diff_lambda, diff_subln, mla_q_norm, mla_w_uq, mla_kv_norm, mla_w_ukv, gdn_conv, gdn_a_log, gdn_dt_bias, gdn_norm, w_branch, w_out, norm_ffn, router_group, router_group_bias, router_expert, router_expert_bias, expert_w_gate, expert_w_up, expert_w_down, norm_final):
    bp, tp, _ = x_prompt.shape
    bs, ts, _ = x_sample.shape
    depth = w_in.shape[0]
    past = cache_diff_k.shape[2]
    n_p, n_s = bp * tp, bs * ts
    tk_s = past + ts

    x = jnp.concatenate([x_prompt.reshape(n_p, D_MODEL), x_sample.reshape(n_s, D_MODEL)], axis=0)
    pos = jnp.concatenate([jnp.tile(jnp.arange(tp), bp), jnp.tile(past + jnp.arange(ts), bs)])
    tabs = (_rope_tables(pos, DH_A, 0, ROT_A // 2)
            + _rope_tables(pos, LANES, NOPE_B, ROPE_B // 2)
            + _rope_tables(pos, LANES, 0, ROPE_B // 2))

    zero_conv = jnp.zeros((bp, CONV_W - 1, QKV_C), F32)
    zero_state = jnp.zeros((bp, H_C, DV_C, DK_C), F32)
    zero_branch = jnp.zeros((n_p + n_s, BRANCH_W), BF16)
    last = np.arange(-(CONV_W - 1), 0)
    tail_p = ((np.arange(bp)[:, None] + 1) * tp + last[None, :]).reshape(-1)
    tail_s = (n_p + (np.arange(bs)[:, None] + 1) * ts + last[None, :]).reshape(-1)
    st_p, st_s = [], []
    w_slab, w_gates = _pack_w_in(w_in)
    w_branch16 = w_branch.astype(BF16)
    w_out16 = w_out.astype(BF16)
    for l in range(depth):
        lam_init = 0.8 - 0.6 * math.exp(-0.3 * l)
        wq, wk, p_kr, wv = _pack_mla(mla_w_uq[l], mla_w_ukv[l])

        h = rmsnorm_rows(x, norm_mix[l], BF16)
        slab = matmul(h, w_slab, l, F32, name="proj_in")
        gates = matmul(h, w_gates, l, BF16, act="sigmoid", name="proj_gates")
        qa, ka, ka_bf, va, va_bf, qb, ckv, kr = mixer_prep(slab, tabs, mla_q_norm[l], wq, mla_kv_norm[l])

        diff_kw = dict(diff=True, lamp=diff_lambda[l], subln=diff_subln[l], lam_init=lam_init)
        oa = attention(qa, ka_bf, va_bf, zero_branch, n_streams=bp, q_row0=0, t_q=tp, t_k=tp, q_off=0,
                       tq=ATTN_TQ, tk=ATTN_TK, name="diff_attn_p", **diff_kw)
        oa = attention_cached(qa, ka_bf, va_bf, cache_diff_k[l].reshape(bs, past, 1024),
                              cache_diff_v[l].reshape(bs, past, 1024), oa, row0=n_p,
                              lamp=diff_lambda[l], subln=diff_subln[l], lam_init=lam_init)

        kb, vb = mla_expand(ckv, kr, wk, p_kr, wv)
        ob = attention(qb, kb, vb, zero_branch, n_streams=bp, q_row0=0, t_q=tp, t_k=tp, q_off=0,
                       tq=ATTN_TQ, tk=ATTN_TK, diff=False, name="mla_attn_p")
        ckv_all = jnp.concatenate([cache_mla_ckv[l], ckv[n_p:].reshape(bs, ts, KV_LORA)], axis=1)
        kr_all = jnp.concatenate([cache_mla_krope[l], kr[n_p:].reshape(bs, ts, ROPE_B)], axis=1)
        kb_s, vb_s = mla_expand(ckv_all.reshape(bs * tk_s, KV_LORA), kr_all.reshape(bs * tk_s, ROPE_B),
                                wk, p_kr, wv)
        ob = attention(qb, kb_s, vb_s, ob, n_streams=bs, q_row0=n_p, t_q=ts, t_k=tk_s, q_off=past,
                       tq=ts, tk=tk_s, diff=False, name="mla_attn_s")

        alog_row = _misc_row(gdn_a_log[l], MISC_A0)
        dtb_row = _misc_row(gdn_dt_bias[l], MISC_A0)
        oc, s_p = gated_delta(slab, zero_conv, zero_state, gdn_conv[l], alog_row, dtb_row, gdn_norm[l], zero_branch,
                              row_blk0=0, n_streams=bp, n_chunks=tp // CHUNK)
        oc, s_s = gated_delta(slab, state_gdn_conv[l], state_gdn_s[l], gdn_conv[l], alog_row, dtb_row,
                              gdn_norm[l], oc, row_blk0=n_p // CHUNK, n_streams=bs, n_chunks=ts // CHUNK)
        conv_p = slab[tail_p][:, COL_CQ:COL_CQ + QKV_C].reshape(bp, CONV_W - 1, QKV_C)
        conv_s = slab[tail_s][:, COL_CQ:COL_CQ + QKV_C].reshape(bs, CONV_W - 1, QKV_C)

        merged = branch_merge(oa, ob, oc, w_branch16, l, gates)
        x = matmul(merged, w_out16, l, F32, residual=x, name="proj_out")

        w_router = jnp.concatenate([router_group[l], router_expert[l],
                                    jnp.zeros((D_MODEL, LANES - N_GROUPS - N_EXPERTS), F32)], axis=1).astype(BF16)
        b_router = jnp.concatenate([router_group_bias[l], router_expert_bias[l],
                                    jnp.zeros((LANES - N_GROUPS - N_EXPERTS,), F32)]).reshape(1, LANES)
        x = hier_moe_residual(x, norm_ffn[l], w_router, b_router, expert_w_gate, expert_w_up, expert_w_down, l)

        st_p.append((ka[:n_p].reshape(bp, tp, H_A, 2, DH_A), va[:n_p].reshape(bp, tp, H_A, VA),
                     ckv[:n_p].reshape(bp, tp, KV_LORA), kr[:n_p].reshape(bp, tp, ROPE_B), conv_p, s_p))
        st_s.append((ka[n_p:].reshape(bs, ts, H_A, 2, DH_A), va[n_p:].reshape(bs, ts, H_A, VA),
                     ckv[n_p:].reshape(bs, ts, KV_LORA), kr[n_p:].reshape(bs, ts, ROPE_B), conv_s, s_s))

    y_p, y_s = rmsnorm_split(x, norm_final, n_p)
    y_prompt = y_p.reshape(bp, tp, D_MODEL)
    y_sample = y_s.reshape(bs, ts, D_MODEL)

    def stk(sts, i):
        return jnp.stack([s[i] for s in sts])

    return (y_prompt, y_sample,
            stk(st_p, 0), stk(st_p, 1), stk(st_p, 2), stk(st_p, 3), stk(st_p, 4), stk(st_p, 5),
            stk(st_s, 0), stk(st_s, 1), stk(st_s, 2), stk(st_s, 3), stk(st_s, 4), stk(st_s, 5))
```

```python
import functools
import math

import jax
import jax.numpy as jnp
import numpy as np
from jax import lax
from jax.experimental import pallas as pl
from jax.experimental.pallas import tpu as pltpu

F32 = jnp.float32
BF16 = jnp.bfloat16

D_MODEL = 2048
CHUNK = 64
CHUNK_SHIFT = 6
ROPE_THETA = 500000.0
EPS = 1e-6
N_BRANCH = 3
H_A = 8
DH_A = 64
VA = 2 * DH_A
ROT_A = DH_A // 4
H_B = 8
Q_LORA = 512
KV_LORA = 256
NOPE_B = 64
ROPE_B = 32
V_B = 128
H_C = 8
DK_C = 128
DV_C = 128
CONV_W = 4
QKV_C = 2 * H_C * DK_C + H_C * DV_C
BRANCH_W = H_A * VA
N_GROUPS = 4
EXPERTS_PER_GROUP = 4
N_EXPERTS = N_GROUPS * EXPERTS_PER_GROUP
D_EXPERT = 512

LANES = 128
VMEM_LIMIT_BYTES = 48 * 1024 * 1024
EXPERT_VMEM_LIMIT_BYTES = 56 * 1024 * 1024
NEG_BIG = -1e30
LOG2_E = math.log2(math.e)

COL_AQ, COL_AK, COL_AV = 0, 1024, 2048
COL_DQ, COL_DKV, COL_MISC = 3072, 3584, 3840
COL_CQ, COL_CK, COL_CV, COL_CZ = 4096, 5120, 6144, 7168
SLAB_F = 8192
MISC_B0 = ROPE_B
MISC_A0 = ROPE_B + H_C

MOE_ROWS = 256
MM_ROWS = 1088
SLAB_ROWS = D_MODEL // LANES
ATTN_TQ = 512
ATTN_TK = 512


def _cparams(*sem):
    return pltpu.CompilerParams(dimension_semantics=sem, vmem_limit_bytes=VMEM_LIMIT_BYTES)


def _sigmoid(x):
    return 1.0 / (1.0 + jnp.exp(-x))


def _silu(x):
    return x * _sigmoid(x)


def _rmsnorm_kernel(x_ref, w_ref, o_ref):
    x = x_ref[...]
    ms = jnp.mean(x * x, axis=-1, keepdims=True)
    o_ref[...] = (x * lax.rsqrt(ms + EPS) * w_ref[...]).astype(o_ref.dtype)


def rmsnorm_rows(x, w, out_dtype, tm=512):
    m, d = x.shape
    return pl.pallas_call(
        _rmsnorm_kernel,
        grid=(m // tm,),
        in_specs=[pl.BlockSpec((tm, d), lambda i: (i, 0)),
                  pl.BlockSpec((1, d), lambda i: (0, 0))],
        out_specs=pl.BlockSpec((tm, d), lambda i: (i, 0)),
        out_shape=jax.ShapeDtypeStruct((m, d), out_dtype),
        compiler_params=_cparams("parallel"),
        name="rmsnorm",
    )(x, w.reshape(1, d))


def _rmsnorm_split_kernel(x_ref, w_ref, o1_ref, o2_ref, *, n1):
    x = x_ref[...]
    y = x * lax.rsqrt(jnp.mean(x * x, axis=-1, keepdims=True) + EPS) * w_ref[...]
    i = pl.program_id(0)

    @pl.when(i < n1)
    def _():
        o1_ref[...] = y

    @pl.when(i >= n1)
    def _():
        o2_ref[...] = y


def rmsnorm_split(x, w, rows1, tm=512):
    m, d = x.shape
    n1 = rows1 // tm
    return pl.pallas_call(
        functools.partial(_rmsnorm_split_kernel, n1=n1),
        grid=(m // tm,),
        in_specs=[pl.BlockSpec((tm, d), lambda i: (i, 0)),
                  pl.BlockSpec((1, d), lambda i: (0, 0))],
        out_specs=[pl.BlockSpec((tm, d), lambda i: (jnp.minimum(i, n1 - 1), 0)),
                   pl.BlockSpec((tm, d), lambda i: (jnp.maximum(i - n1, 0), 0))],
        out_shape=[jax.ShapeDtypeStruct((rows1, d), F32), jax.ShapeDtypeStruct((m - rows1, d), F32)],
        compiler_params=_cparams("arbitrary"),
        name="rmsnorm_final",
    )(x, w.reshape(1, d))


def _mm_kernel(a_ref, b_ref, o_ref, *, act):
    acc = jnp.dot(a_ref[...], b_ref[...], preferred_element_type=F32)
    if act == "sigmoid":
        acc = _sigmoid(acc)
    o_ref[...] = acc.astype(o_ref.dtype)


def _mm_res_kernel(a_ref, b_ref, r_ref, o_ref):
    acc = jnp.dot(a_ref[...], b_ref[...], preferred_element_type=F32)
    o_ref[...] = r_ref[...] + acc


def matmul(a, b, layer, out_dtype, act=None, residual=None, tm=MM_ROWS, tn=1024, name="mm"):
    m, k = a.shape
    n = b.shape[2]
    in_specs = [pl.BlockSpec((tm, k), lambda j, i: (i, 0)),
                pl.BlockSpec((None, k, tn), lambda j, i: (layer, 0, j))]
    args = [a, b]
    if residual is None:
        body = functools.partial(_mm_kernel, act=act)
    else:
        body = _mm_res_kernel
        in_specs.append(pl.BlockSpec((tm, tn), lambda j, i: (i, j)))
        args.append(residual)
    return pl.pallas_call(
        body,
        grid=(n // tn, m // tm),
        in_specs=in_specs,
        out_specs=pl.BlockSpec((tm, tn), lambda j, i: (i, j)),
        out_shape=jax.ShapeDtypeStruct((m, n), out_dtype),
        compiler_params=_cparams("parallel", "parallel"),
        name=name,
    )(*args)


def _rope_tables(pos, period, offset, half):
    rot = 2 * half
    inv = ROPE_THETA ** (-jnp.arange(half, dtype=F32) * 2.0 / rot)
    ang = pos.astype(F32)[:, None] * inv[None, :]
    cos, sin = jnp.cos(ang), jnp.sin(ang)
    lane = jnp.arange(LANES)
    r = (lane % period) - offset
    first = (r >= 0) & (r < half)
    second = (r >= half) & (r < rot)
    idx = jnp.clip(jnp.where(second, r - half, r), 0, half - 1)
    cos_l, sin_l = cos[:, idx], sin[:, idx]
    c = jnp.where((first | second)[None, :], cos_l, 1.0)
    sa = jnp.where(first[None, :], -sin_l, 0.0)
    sb = jnp.where(second[None, :], sin_l, 0.0)
    return c, sa, sb


def _rope_lanes(x, c, sa, sb, half):
    return (x * c + pltpu.roll(x, LANES - half, axis=1) * sa
            + pltpu.roll(x, half, axis=1) * sb)


def _prep_kernel(aq_ref, ak_ref, av_ref, dq_ref, dkv_ref, misc_ref,
                 ca_ref, saa_ref, sba_ref, cb_ref, sab_ref, sbb_ref, ck_ref, sak_ref, sbk_ref,
                 qn_ref, wuq_ref, kvn_ref,
                 qa_o, ka_o, kabf_o, va_o, vabf_o, qb_o, ckv_o, kr_o):
    ca, saa, sba = ca_ref[...], saa_ref[...], sba_ref[...]
    for c in range(H_A):
        sl = slice(c * LANES, (c + 1) * LANES)
        q = _rope_lanes(aq_ref[:, sl], ca, saa, sba, ROT_A // 2)
        qa_o[:, sl] = (q * (DH_A ** -0.5 * LOG2_E)).astype(BF16)
        k = _rope_lanes(ak_ref[:, sl], ca, saa, sba, ROT_A // 2)
        ka_o[:, sl] = k
        kabf_o[:, sl] = k.astype(BF16)
    v = av_ref[...]
    va_o[...] = v
    vabf_o[...] = v.astype(BF16)

    x = dq_ref[...]
    cq = x * lax.rsqrt(jnp.mean(x * x, axis=-1, keepdims=True) + EPS) * qn_ref[...]
    qb = jnp.dot(cq.astype(BF16), wuq_ref[...], preferred_element_type=F32)
    cb, sab, sbb = cb_ref[...], sab_ref[...], sbb_ref[...]
    scale_b = (NOPE_B + ROPE_B) ** -0.5 * LOG2_E
    for c in range(H_B):
        sl = slice(c * LANES, (c + 1) * LANES)
        q = _rope_lanes(qb[:, sl], cb, sab, sbb, ROPE_B // 2)
        qb_o[:, sl] = (q * scale_b).astype(BF16)

    x = dkv_ref[...]
    ckv_o[...] = x * lax.rsqrt(jnp.mean(x * x, axis=-1, keepdims=True) + EPS) * kvn_ref[...]
    kr = _rope_lanes(misc_ref[...], ck_ref[...], sak_ref[...], sbk_ref[...], ROPE_B // 2)
    kr_o[...] = kr[:, :ROPE_B]


def mixer_prep(slab, tabs, q_norm, w_uq_pad, kv_norm, tm=256):
    m = slab.shape[0]

    def col(width, start):
        blk = start // width
        return pl.BlockSpec((tm, width), lambda i: (i, blk))

    def row(width):
        return pl.BlockSpec((tm, width), lambda i: (i, 0))

    def full(shape):
        return pl.BlockSpec(shape, lambda i: (0,) * len(shape))

    in_specs = [col(1024, COL_AQ), col(1024, COL_AK), col(1024, COL_AV),
                col(Q_LORA, COL_DQ), col(KV_LORA, COL_DKV), col(LANES, COL_MISC)]
    in_specs += [row(LANES)] * 9
    in_specs += [full((1, Q_LORA)), full((Q_LORA, 1024)), full((1, KV_LORA))]
    out_shape = [jax.ShapeDtypeStruct((m, 1024), BF16),
                 jax.ShapeDtypeStruct((m, 1024), F32),
                 jax.ShapeDtypeStruct((m, 1024), BF16),
                 jax.ShapeDtypeStruct((m, 1024), F32),
                 jax.ShapeDtypeStruct((m, 1024), BF16),
                 jax.ShapeDtypeStruct((m, 1024), BF16),
                 jax.ShapeDtypeStruct((m, KV_LORA), F32),
                 jax.ShapeDtypeStruct((m, ROPE_B), F32)]
    out_specs = [row(1024)] * 6 + [row(KV_LORA), row(ROPE_B)]
    return pl.pallas_call(
        _prep_kernel,
        grid=(m // tm,),
        in_specs=in_specs,
        out_specs=out_specs,
        out_shape=out_shape,
        compiler_params=_cparams("parallel"),
        name="mixer_prep",
    )(slab, slab, slab, slab, slab, slab, *tabs,
      q_norm.reshape(1, Q_LORA), w_uq_pad, kv_norm.reshape(1, KV_LORA))


def _expand_kernel(ckv_ref, kr_ref, wuk_ref, pk_ref, wuv_ref, kb_o, vb_o):
    ckv = ckv_ref[...].astype(BF16)
    kr = kr_ref[...].astype(BF16)
    kb = (jnp.dot(ckv, wuk_ref[...], preferred_element_type=F32)
          + jnp.dot(kr, pk_ref[...], preferred_element_type=F32))
    kb_o[...] = kb.astype(BF16)
    vb_o[...] = jnp.dot(ckv, wuv_ref[...], preferred_element_type=F32).astype(BF16)


def mla_expand(ckv, kr, w_uk_pad, p_kr, w_uv, tm=512):
    m = ckv.shape[0]
    return pl.pallas_call(
        _expand_kernel,
        grid=(m // tm,),
        in_specs=[pl.BlockSpec((tm, KV_LORA), lambda i: (i, 0)),
                  pl.BlockSpec((tm, ROPE_B), lambda i: (i, 0)),
                  pl.BlockSpec((KV_LORA, 1024), lambda i: (0, 0)),
                  pl.BlockSpec((ROPE_B, 1024), lambda i: (0, 0)),
                  pl.BlockSpec((KV_LORA, 1024), lambda i: (0, 0))],
        out_specs=[pl.BlockSpec((tm, 1024), lambda i: (i, 0))] * 2,
        out_shape=[jax.ShapeDtypeStruct((m, 1024), BF16)] * 2,
        compiler_params=_cparams("parallel"),
        name="mla_expand",
    )(ckv, kr, w_uk_pad, p_kr, w_uv)


def _attn_kernel(*refs, diff, tq, tk, q_off, nk, lam_init, aliased, group):
    q_ref, k_ref, v_ref = refs[:3]
    pos = 3
    if diff:
        lamp_ref, subln_ref = refs[3:5]
        pos = 5
    if aliased:
        pos += 1
    single = nk == 1
    if single:
        o_ref, bias_sc = refs[pos:pos + 2]
    else:
        o_ref, bias_sc, m_sc, l_sc, acc_sc = refs[pos:pos + 5]
    n_maps = 2 if diff else 1
    maps = [(h, mp) for h in range(H_A) for mp in range(n_maps)]
    i = pl.program_id(1)
    j = pl.program_id(2)

    def write_head(h, outs):
        if diff:
            lp = lamp_ref[...]
            lam = (jnp.exp(jnp.sum(lp[0:1] * lp[1:2], axis=-1, keepdims=True))
                   - jnp.exp(jnp.sum(lp[2:3] * lp[3:4], axis=-1, keepdims=True)) + lam_init)
            o = outs[0] - lam * outs[1]
            o = o * lax.rsqrt(jnp.mean(o * o, axis=-1, keepdims=True) + EPS) * subln_ref[...]
            o = o * (1.0 - lam_init)
        else:
            o = outs[0]
        o_ref[:, h * LANES:(h + 1) * LANES] = o.astype(o_ref.dtype)

    if not single:
        @pl.when(j == 0)
        def _():
            m_sc[...] = jnp.full(m_sc.shape, NEG_BIG, F32)
            l_sc[...] = jnp.zeros(l_sc.shape, F32)
            acc_sc[...] = jnp.zeros(acc_sc.shape, F32)

    q_chunk_max = (q_off + i * tq + tq - 1) // CHUNK

    @pl.when((j * tk) // CHUNK <= q_chunk_max)
    def _():
        q_chunk = lax.shift_right_logical(q_off + i * tq + lax.broadcasted_iota(jnp.int32, (tq, 1), 0),
                                          CHUNK_SHIFT)
        k_chunk = lax.shift_right_logical(j * tk + lax.broadcasted_iota(jnp.int32, (1, tk), 1), CHUNK_SHIFT)
        bias_sc[...] = jnp.where(q_chunk >= k_chunk, 0.0, NEG_BIG)
        lane = lax.broadcasted_iota(jnp.int32, (1, LANES), 1)

        def scores(h, mp):
            hs = slice(h * LANES, (h + 1) * LANES)
            q = q_ref[:, hs]
            if diff:
                in_map = (lane < DH_A) if mp == 0 else (lane >= DH_A)
                q = jnp.where(in_map, q, jnp.zeros_like(q))
            return lax.dot_general(q, k_ref[:, hs], (((1,), (1,)), ((), ())),
                                   preferred_element_type=F32) + bias_sc[...]

        def pv(p, h):
            return jnp.dot(p.astype(BF16), v_ref[:, h * LANES:(h + 1) * LANES], preferred_element_type=F32)

        for g0 in range(0, len(maps), group):
            grp = maps[g0:g0 + group]
            s = [scores(h, mp) for h, mp in grp]
            m_cur = [jnp.max(x, axis=-1, keepdims=True) for x in s]
            if single:
                p = [jnp.exp2(x - m) for x, m in zip(s, m_cur)]
                l = [jnp.sum(x, axis=-1, keepdims=True) for x in p]
                outs = [pv(x, h) / d for x, d, (h, _) in zip(p, l, grp)]
                for t in range(0, len(grp), n_maps):
                    write_head(grp[t][0], outs[t:t + n_maps])
            else:
                ids = [h * n_maps + mp for h, mp in grp]
                m_prev = [m_sc[t] for t in ids]
                m_new = [jnp.maximum(a, b) for a, b in zip(m_prev, m_cur)]
                p = [jnp.exp2(x - jnp.concatenate([m] * (tk // LANES), axis=1)) for x, m in zip(s, m_new)]
                alpha = [jnp.exp2(a - b) for a, b in zip(m_prev, m_new)]
                for t, a, x, m, (h, _) in zip(ids, alpha, p, m_new, grp):
                    l_sc[t] = a * l_sc[t] + jnp.sum(x, axis=-1, keepdims=True)
                    acc_sc[t] = a * acc_sc[t] + pv(x, h)
                    m_sc[t] = m

    if not single:
        @pl.when(j == nk - 1)
        def _():
            for h in range(H_A):
                write_head(h, [acc_sc[h * n_maps + mp] / l_sc[h * n_maps + mp] for mp in range(n_maps)])


def attention(q, k, v, out_prev, *, n_streams, q_row0, t_q, t_k, q_off, tq, tk, diff,
              lamp=None, subln=None, lam_init=0.0, group=2, name="attn"):
    nq, nk = t_q // tq, t_k // tk
    qb0 = q_row0 // tq
    width = H_A * LANES

    def q_map(b, i, j):
        return (qb0 + b * nq + i, 0)

    def kv_map(b, i, j):
        last = ((q_off + i * tq + tq - 1) // CHUNK * CHUNK) // tk
        return (b * nk + jnp.minimum(j, last), 0)

    in_specs = [pl.BlockSpec((tq, width), q_map),
                pl.BlockSpec((tk, width), kv_map),
                pl.BlockSpec((tk, width), kv_map)]
    args = [q, k, v]
    n_maps = 1
    if diff:
        n_maps = 2
        in_specs += [pl.BlockSpec((4, DH_A), lambda b, i, j: (0, 0)),
                     pl.BlockSpec((1, VA), lambda b, i, j: (0, 0))]
        args += [lamp, subln.reshape(1, VA)]
    aliases = {}
    if out_prev is not None:
        aliases = {len(args): 0}
        in_specs.append(pl.BlockSpec(memory_space=pl.ANY))
        args.append(out_prev)
    return pl.pallas_call(
        functools.partial(_attn_kernel, diff=diff, tq=tq, tk=tk, q_off=q_off, nk=nk, lam_init=lam_init,
                          aliased=out_prev is not None, group=group),
        grid=(n_streams, nq, nk),
        in_specs=in_specs,
        out_specs=pl.BlockSpec((tq, width), q_map),
        out_shape=jax.ShapeDtypeStruct((q.shape[0], width), BF16),
        scratch_shapes=[pltpu.VMEM((tq, tk), F32)] + ([] if nk == 1 else
                                                      [pltpu.VMEM((H_A * n_maps, tq, LANES), F32)] * 3),
        input_output_aliases=aliases,
        compiler_params=_cparams("parallel", "parallel", "arbitrary"),
        name=name,
    )(*args)


def _attn_cached_kernel(q_ref, kn_ref, vn_ref, kc_ref, vc_ref, lamp_ref, subln_ref, prev_ref, o_ref, bias_sc,
                        *, ts, past, lam_init):
    del prev_ref
    pos = past + lax.broadcasted_iota(jnp.int32, (ts, 1), 0)
    kpos = past + lax.broadcasted_iota(jnp.int32, (1, ts), 1)
    bias_sc[...] = jnp.where(lax.shift_right_logical(pos, CHUNK_SHIFT)
                             >= lax.shift_right_logical(kpos, CHUNK_SHIFT), 0.0, NEG_BIG)
    lane = lax.broadcasted_iota(jnp.int32, (1, LANES), 1)
    lp = lamp_ref[...]
    lam = (jnp.exp(jnp.sum(lp[0:1] * lp[1:2], axis=-1, keepdims=True))
           - jnp.exp(jnp.sum(lp[2:3] * lp[3:4], axis=-1, keepdims=True)) + lam_init)
    nt = (((1,), (1,)), ((), ()))
    for h in range(H_A):
        hs = slice(h * LANES, (h + 1) * LANES)
        q = q_ref[:, hs]
        qs = [jnp.where(lane < DH_A, q, jnp.zeros_like(q)), jnp.where(lane >= DH_A, q, jnp.zeros_like(q))]
        kc = kc_ref[0, :, hs].astype(BF16)
        vc = vc_ref[0, :, hs].astype(BF16)
        kn = kn_ref[:, hs]
        vn = vn_ref[:, hs]
        s_old = [lax.dot_general(x, kc, nt, preferred_element_type=F32) for x in qs]
        s_new = [lax.dot_general(x, kn, nt, preferred_element_type=F32) + bias_sc[...] for x in qs]
        m = [jnp.maximum(jnp.max(a, axis=-1, keepdims=True), jnp.max(b, axis=-1, keepdims=True))
             for a, b in zip(s_old, s_new)]
        p_old = [jnp.exp2(a - t) for a, t in zip(s_old, m)]
        p_new = [jnp.exp2(b - t) for b, t in zip(s_new, m)]
        l = [jnp.sum(a, axis=-1, keepdims=True) + jnp.sum(b, axis=-1, keepdims=True)
             for a, b in zip(p_old, p_new)]
        outs = [(jnp.dot(a.astype(BF16), vc, preferred_element_type=F32)
                 + jnp.dot(b.astype(BF16), vn, preferred_element_type=F32)) / d
                for a, b, d in zip(p_old, p_new, l)]
        o = outs[0] - lam * outs[1]
        o = o * lax.rsqrt(jnp.mean(o * o, axis=-1, keepdims=True) + EPS) * subln_ref[...]
        o_ref[:, hs] = (o * (1.0 - lam_init)).astype(o_ref.dtype)


def attention_cached(q, k_new, v_new, cache_k, cache_v, out_prev, *, row0, lamp, subln, lam_init):
    bs, past, width = cache_k.shape
    ts = (q.shape[0] - row0) // bs
    blk0 = row0 // ts
    new_spec = pl.BlockSpec((ts, width), lambda b: (blk0 + b, 0))
    cache_spec = pl.BlockSpec((1, past, width), lambda b: (b, 0, 0))
    return pl.pallas_call(
        functools.partial(_attn_cached_kernel, ts=ts, past=past, lam_init=lam_init),
        grid=(bs,),
        in_specs=[new_spec, new_spec, new_spec, cache_spec, cache_spec,
                  pl.BlockSpec((4, DH_A), lambda b: (0, 0)),
                  pl.BlockSpec((1, VA), lambda b: (0, 0)),
                  pl.BlockSpec(memory_space=pl.ANY)],
        out_specs=new_spec,
        out_shape=jax.ShapeDtypeStruct(out_prev.shape, BF16),
        scratch_shapes=[pltpu.VMEM((ts, ts), F32)],
        input_output_aliases={7: 0},
        compiler_params=_cparams("parallel"),
        name="diff_attn_s",
    )(q, k_new, v_new, cache_k, cache_v, lamp, subln.reshape(1, VA), out_prev)


def _mm_nt(a, b):
    return lax.dot_general(a, b, (((1,), (1,)), ((), ())), preferred_element_type=F32)


def _mm_nn(a, b):
    return jnp.dot(a, b, preferred_element_type=F32)


def _mm_tn(a, b):
    return lax.dot_general(a, b, (((0,), (0,)), ((), ())), preferred_element_type=F32)


def _bf(xs):
    return [x.astype(BF16) for x in xs]


def _shift_rows(x, tail, s):
    xs = pltpu.roll(x, s, axis=0)
    ts = pltpu.roll(tail, s, axis=0)
    row = lax.broadcasted_iota(jnp.int32, (8, 1), 0)
    top = jnp.where(row < s, ts, xs[0:8])
    return jnp.concatenate([top, xs[8:]], axis=0)


def _gdn_kernel(*refs, n_chunks, aliased):
    (cq_ref, ck_ref, cv_ref, cz_ref, misc_ref, convp_ref, s0_ref,
     kern_ref, alog_ref, dtb_ref, norm_ref) = refs[:11]
    o_ref, s_o, s_sc, tail_sc = refs[(12 if aliased else 11):]
    n = pl.program_id(1)
    L = CHUNK
    heads = range(H_C)

    @pl.when(n == 0)
    def _():
        s_sc[...] = s0_ref[0]
        tail_sc[...] = jnp.zeros(tail_sc.shape, F32)
        tail_sc[8 - (CONV_W - 1):8, :] = convp_ref[0]

    conv = []
    for part, ref in enumerate((cq_ref, ck_ref, cv_ref)):
        cs = slice(part * 1024, (part + 1) * 1024)
        x = ref[...]
        tail = tail_sc[:, cs]
        acc = x * kern_ref[CONV_W - 1:CONV_W, cs]
        for s in range(1, CONV_W):
            acc = acc + _shift_rows(x, tail, s) * kern_ref[CONV_W - 1 - s:CONV_W - s, cs]
        conv.append(_silu(acc))
        tail_sc[:, cs] = x[L - 8:L]
    qc, kc, vc = conv

    misc = misc_ref[...]
    beta_all = _sigmoid(misc)
    za = misc + dtb_ref[...]
    softplus = jnp.maximum(za, 0.0) + jnp.log(1.0 + jnp.exp(-jnp.abs(za)))
    g_all = -jnp.exp(alog_ref[...]) * softplus
    row = lax.broadcasted_iota(jnp.int32, (L, 1), 0)
    gcum_all = g_all
    for s in (1, 2, 4, 8, 16, 32):
        gcum_all = gcum_all + jnp.where(row >= s, pltpu.roll(gcum_all, s, axis=0), 0.0)
    lane = lax.broadcasted_iota(jnp.int32, (1, LANES), 1)
    is_beta = (lane >= MISC_B0) & (lane < MISC_B0 + H_C)
    packed = jnp.where(is_beta, beta_all, gcum_all)
    packed_t = jnp.concatenate([packed, jnp.zeros((LANES - L, LANES), F32)], axis=0).T

    ri = lax.broadcasted_iota(jnp.int32, (L, L), 0)
    ci = lax.broadcasted_iota(jnp.int32, (L, L), 1)
    incl = ri >= ci
    strict = ri > ci
    diag16 = (ri // 16) == (ci // 16)

    qn, kn, v_h, beta, gcum, decay, gam = [], [], [], [], [], [], []
    for h in heads:
        hs = slice(h * LANES, (h + 1) * LANES)
        qh, kh = qc[:, hs], kc[:, hs]
        qn.append(qh * lax.rsqrt(jnp.sum(qh * qh, axis=-1, keepdims=True) + EPS) * (DK_C ** -0.5))
        kn.append(kh * lax.rsqrt(jnp.sum(kh * kh, axis=-1, keepdims=True) + EPS))
        v_h.append(vc[:, hs])
        beta.append(beta_all[:, MISC_B0 + h:MISC_B0 + h + 1])
        gc = gcum_all[:, MISC_A0 + h:MISC_A0 + h + 1]
        gcum.append(gc)
        seg = gc - packed_t[MISC_A0 + h:MISC_A0 + h + 1, 0:L]
        decay.append(jnp.where(incl, jnp.exp(jnp.where(incl, seg, 0.0)), 0.0))
        gam.append(jnp.exp(gc))
    q16, k16 = _bf(qn), _bf(kn)
    kk = [_mm_nt(k16[h], k16[h]) for h in heads]
    pq = [_mm_nt(q16[h], k16[h]) * decay[h] for h in heads]
    a_mat = [jnp.where(strict, beta[h] * kk[h] * decay[h], 0.0) for h in heads]

    a_d = [jnp.where(diag16, a_mat[h], 0.0) for h in heads]
    a_lo = [a_mat[h] - a_d[h] for h in heads]
    ad16 = _bf(a_d)
    a2 = [_mm_nn(ad16[h], ad16[h]) for h in heads]
    a2_16 = _bf(a2)
    a4 = [_mm_nn(a2_16[h], a2_16[h]) for h in heads]
    a4_16 = _bf(a4)
    a8 = [_mm_nn(a4_16[h], a4_16[h]) for h in heads]
    f12 = [a2[h] - a_d[h] - _mm_nn(ad16[h], a2_16[h]) for h in heads]
    a8_16 = _bf(a8)
    f34 = [a4[h] + a8[h] + _mm_nn(a4_16[h], a8_16[h]) for h in heads]
    f12_16, f34_16 = _bf(f12), _bf(f34)
    g = [f12[h] + f34[h] + _mm_nn(f12_16[h], f34_16[h]) for h in heads]
    g16, alo16 = _bf(g), _bf(a_lo)
    nm = [a_lo[h] + _mm_nn(g16[h], alo16[h]) for h in heads]
    n16 = _bf(nm)
    n2 = [_mm_nn(n16[h], n16[h]) for h in heads]
    n2_16 = _bf(n2)
    hm = [n2[h] - nm[h] - _mm_nn(n16[h], n2_16[h]) for h in heads]
    hm16 = _bf(hm)
    tm1 = [hm[h] + g[h] + _mm_nn(hm16[h], g16[h]) for h in heads]
    tm1_16 = _bf(tm1)
    rhs = [jnp.concatenate([v_h[h] * beta[h], kn[h] * (beta[h] * gam[h])], axis=-1) for h in heads]
    rhs16 = _bf(rhs)
    sol = [rhs[h] + _mm_nn(tm1_16[h], rhs16[h]) for h in heads]

    s_prev = [s_sc[h] for h in heads]
    s16 = _bf(s_prev)
    w16 = _bf([sol[h][:, DV_C:] for h in heads])
    u = [sol[h][:, :DV_C] - _mm_nt(w16[h], s16[h]) for h in heads]
    u16 = _bf(u)
    pq16 = _bf(pq)
    o = [gam[h] * _mm_nt(q16[h], s16[h]) + _mm_nn(pq16[h], u16[h]) for h in heads]
    for h in heads:
        gcum_end = gcum[h][L - 1:L, :]
        ut = (u[h] * jnp.exp(gcum_end - gcum[h])).astype(BF16)
        s_sc[h] = jnp.exp(gcum_end) * s_prev[h] + _mm_tn(ut, k16[h])
    for h in heads:
        hs = slice(h * LANES, (h + 1) * LANES)
        oh = o[h] * lax.rsqrt(jnp.mean(o[h] * o[h], axis=-1, keepdims=True) + EPS) * norm_ref[...]
        o_ref[:, hs] = (oh * _silu(cz_ref[:, hs])).astype(o_ref.dtype)

    @pl.when(n == n_chunks - 1)
    def _():
        s_o[0] = s_sc[...]


def gated_delta(slab, conv_prev, s0, kern, alog_row, dtb_row, norm_w, out_prev, *, row_blk0, n_streams, n_chunks):
    def rows(b, n):
        return row_blk0 + b * n_chunks + n

    def col(start):
        blk = start // 1024
        return pl.BlockSpec((CHUNK, 1024), lambda b, n: (rows(b, n), blk))

    def full(shape):
        return pl.BlockSpec(shape, lambda b, n: (0,) * len(shape))

    in_specs = [col(COL_CQ), col(COL_CK), col(COL_CV), col(COL_CZ),
                pl.BlockSpec((CHUNK, LANES), lambda b, n: (rows(b, n), COL_MISC // LANES)),
                pl.BlockSpec((1, CONV_W - 1, QKV_C), lambda b, n: (b, 0, 0)),
                pl.BlockSpec((1, H_C, DV_C, DK_C), lambda b, n: (b, 0, 0, 0)),
                full((CONV_W, QKV_C)), full((1, LANES)), full((1, LANES)), full((1, DV_C))]
    args = [slab, slab, slab, slab, slab, conv_prev, s0, kern, alog_row, dtb_row, norm_w.reshape(1, DV_C)]
    aliases = {}
    if out_prev is not None:
        aliases = {len(args): 0}
        in_specs.append(pl.BlockSpec(memory_space=pl.ANY))
        args.append(out_prev)
    return pl.pallas_call(
        functools.partial(_gdn_kernel, n_chunks=n_chunks, aliased=out_prev is not None),
        grid=(n_streams, n_chunks),
        in_specs=in_specs,
        out_specs=[pl.BlockSpec((CHUNK, 1024), lambda b, n: (rows(b, n), 0)),
                   pl.BlockSpec((1, H_C, DV_C, DK_C), lambda b, n: (b, 0, 0, 0))],
        out_shape=[jax.ShapeDtypeStruct((slab.shape[0], 1024), BF16),
                   jax.ShapeDtypeStruct((n_streams, H_C, DV_C, DK_C), F32)],
        scratch_shapes=[pltpu.VMEM((H_C, DV_C, DK_C), F32),
                        pltpu.VMEM((8, QKV_C), F32)],
        input_output_aliases=aliases,
        compiler_params=_cparams("parallel", "arbitrary"),
        name="gated_delta",
    )(*args)


def _merge_kernel(oa_ref, ob_ref, oc_ref, wb_ref, ga_ref, gb_ref, gc_ref, o_ref):
    acc = jnp.dot(oa_ref[...], wb_ref[0], preferred_element_type=F32) * ga_ref[...].astype(F32)
    acc = acc + jnp.dot(ob_ref[...], wb_ref[1], preferred_element_type=F32) * gb_ref[...].astype(F32)
    acc = acc + jnp.dot(oc_ref[...], wb_ref[2], preferred_element_type=F32) * gc_ref[...].astype(F32)
    o_ref[...] = acc.astype(o_ref.dtype)


def branch_merge(oa, ob, oc, w_branch, layer, gates, tm=MM_ROWS // 2, tn=1024):
    m = oa.shape[0]
    nj = D_MODEL // tn

    def gate_spec(nb):
        return pl.BlockSpec((tm, tn), lambda j, i: (i, nb * nj + j))

    return pl.pallas_call(
        _merge_kernel,
        grid=(nj, m // tm),
        in_specs=[pl.BlockSpec((tm, BRANCH_W), lambda j, i: (i, 0))] * 3
                 + [pl.BlockSpec((None, N_BRANCH, BRANCH_W, tn), lambda j, i: (layer, 0, 0, j))]
                 + [gate_spec(0), gate_spec(1), gate_spec(2)],
        out_specs=pl.BlockSpec((tm, tn), lambda j, i: (i, j)),
        out_shape=jax.ShapeDtypeStruct((m, D_MODEL), BF16),
        compiler_params=_cparams("parallel", "parallel"),
        name="branch_merge",
    )(oa, ob, oc, w_branch, gates, gates, gates)


def _slab_store(ref, x, rows):
    for c in range(SLAB_ROWS):
        ref[pl.ds(c, rows, stride=SLAB_ROWS), :] = x[:, c * LANES:(c + 1) * LANES]


def _slab_load(ref, rows):
    return jnp.concatenate([ref[pl.ds(c, rows, stride=SLAB_ROWS), :] for c in range(SLAB_ROWS)], axis=1)


def _router_kernel(x_ref, nw_ref, wrh_ref, wrl_ref, br_ref, h_o, info_o, cnt_o, cnt_sc, tri_sc, *, tm):
    step = pl.program_id(0)

    @pl.when(step == 0)
    def _():
        cnt_sc[...] = jnp.zeros(cnt_sc.shape, F32)
        ri = lax.broadcasted_iota(jnp.int32, (tm, tm), 0)
        ci = lax.broadcasted_iota(jnp.int32, (tm, tm), 1)
        tri_sc[...] = (ri > ci).astype(BF16)

    x = x_ref[...]
    h = (x * lax.rsqrt(jnp.mean(x * x, axis=-1, keepdims=True) + EPS) * nw_ref[...])
    _slab_store(h_o, h, tm)
    h_hi = h.astype(BF16)
    h_lo = (h - h_hi.astype(F32)).astype(BF16)
    logits = (jnp.dot(h_hi, wrh_ref[...], preferred_element_type=F32)
              + jnp.dot(h_hi, wrl_ref[...], preferred_element_type=F32)
              + jnp.dot(h_lo, wrh_ref[...], preferred_element_type=F32)) + br_ref[...]
    lane = lax.broadcasted_iota(jnp.int32, logits.shape, 1)
    is_g = lane < N_GROUPS
    lg = jnp.where(is_g, logits, NEG_BIG)
    eg = jnp.where(is_g, jnp.exp(lg - jnp.max(lg, axis=-1, keepdims=True)), 0.0)
    gp = eg / jnp.sum(eg, axis=-1, keepdims=True)
    pg = jnp.max(gp, axis=-1, keepdims=True)
    grp = jnp.min(jnp.where(is_g & (gp == pg), lane, LANES), axis=-1, keepdims=True)
    lo = N_GROUPS + grp * EXPERTS_PER_GROUP
    is_e = (lane >= lo) & (lane < lo + EXPERTS_PER_GROUP)
    le = jnp.where(is_e, logits, NEG_BIG)
    ee = jnp.where(is_e, jnp.exp(le - jnp.max(le, axis=-1, keepdims=True)), 0.0)
    ep = ee / jnp.sum(ee, axis=-1, keepdims=True)
    v1 = jnp.max(jnp.where(is_e, ep, -1.0), axis=-1, keepdims=True)
    i1 = jnp.min(jnp.where(is_e & (ep == v1), lane, LANES), axis=-1, keepdims=True)
    rest = is_e & (lane != i1)
    v2 = jnp.max(jnp.where(rest, ep, -1.0), axis=-1, keepdims=True)
    i2 = jnp.min(jnp.where(rest & (ep == v2), lane, LANES), axis=-1, keepdims=True)
    den = v1 + v2
    w1 = pg * v1 / den
    w2 = pg * v2 / den

    oh1 = lane == i1
    oh2 = lane == i2
    picked = (oh1 | oh2).astype(F32)
    before = jnp.dot(tri_sc[...], picked.astype(BF16), preferred_element_type=F32) + cnt_sc[...]
    rank1 = jnp.sum(jnp.where(oh1, before, 0.0), axis=-1, keepdims=True)
    rank2 = jnp.sum(jnp.where(oh2, before, 0.0), axis=-1, keepdims=True)
    cnt = cnt_sc[...] + jnp.sum(picked, axis=0, keepdims=True)
    cnt_sc[...] = cnt
    cnt_o[...] = cnt

    info = jnp.where(lane == 0, (i1 - N_GROUPS).astype(F32), 0.0)
    info = jnp.where(lane == 1, (i2 - N_GROUPS).astype(F32), info)
    info = jnp.where(lane == 2, w1, info)
    info = jnp.where(lane == 3, w2, info)
    info = jnp.where(lane == 4, rank1, info)
    info = jnp.where(lane == 5, rank2, info)
    info_o[...] = info


def moe_router(x, norm_w, w_router, b_router, tm=512):
    m = x.shape[0]
    w_hi = w_router.astype(BF16)
    w_lo = (w_router - w_hi.astype(F32)).astype(BF16)
    return pl.pallas_call(
        functools.partial(_router_kernel, tm=tm),
        grid=(m // tm,),
        in_specs=[pl.BlockSpec((tm, D_MODEL), lambda i: (i, 0)),
                  pl.BlockSpec((1, D_MODEL), lambda i: (0, 0)),
                  pl.BlockSpec((D_MODEL, LANES), lambda i: (0, 0)),
                  pl.BlockSpec((D_MODEL, LANES), lambda i: (0, 0)),
                  pl.BlockSpec((1, LANES), lambda i: (0, 0))],
        out_specs=[pl.BlockSpec((tm * SLAB_ROWS, LANES), lambda i: (i, 0)),
                   pl.BlockSpec((tm, LANES), lambda i: (i, 0)),
                   pl.BlockSpec((1, LANES), lambda i: (0, 0))],
        out_shape=[jax.ShapeDtypeStruct((m * SLAB_ROWS, LANES), F32),
                   jax.ShapeDtypeStruct((m, LANES), F32),
                   jax.ShapeDtypeStruct((1, LANES), F32)],
        scratch_shapes=[pltpu.VMEM((1, LANES), F32), pltpu.VMEM((tm, tm), BF16)],
        compiler_params=_cparams("arbitrary"),
        name="moe_router",
    )(x, norm_w.reshape(1, D_MODEL), w_hi, w_lo, b_router)


def _slab_copy(src_ref, dst_ref, sem, tok, r):
    return pltpu.make_async_copy(src_ref.at[pl.ds(tok * SLAB_ROWS, SLAB_ROWS), :],
                                 dst_ref.at[pl.ds(r * SLAB_ROWS, SLAB_ROWS), :], sem)


def _gather_start(src_ref, dst_ref, sem, idx_ref, rows):
    def body(r2, c):
        for u in range(2):
            r = r2 * 2 + u
            _slab_copy(src_ref, dst_ref, sem, idx_ref[0, 0, r], r).start(priority=u)
        return c

    lax.fori_loop(0, rows // 2, body, 0, unroll=4)


def _gather_wait(src_ref, dst_ref, sem, rows):
    def body(r, c):
        _slab_copy(src_ref, dst_ref, sem, 0, r).wait()
        return c

    lax.fori_loop(0, rows, body, 0, unroll=8)


def _combine_kernel(i1c_ref, i2c_ref, i1n_ref, i2n_ref, src_ref, info_ref, res_ref, o_ref, buf1, buf2, sem,
                    *, rows):
    step = pl.program_id(0)
    slot = lax.rem(step, 2)

    def start(i1_ref, i2_ref, s):
        _gather_start(src_ref, buf1.at[s], sem.at[0, s], i1_ref, rows)
        _gather_start(src_ref, buf2.at[s], sem.at[1, s], i2_ref, rows)

    @pl.when(step == 0)
    def _():
        start(i1c_ref, i2c_ref, 0)

    @pl.when(step + 1 < pl.num_programs(0))
    def _():
        start(i1n_ref, i2n_ref, 1 - slot)

    _gather_wait(src_ref, buf1.at[slot], sem.at[0, slot], rows)
    _gather_wait(src_ref, buf2.at[slot], sem.at[1, slot], rows)
    info = info_ref[...]
    o_ref[...] = (res_ref[...] + info[:, 2:3] * _slab_load(buf1.at[slot], rows)
                  + info[:, 3:4] * _slab_load(buf2.at[slot], rows))


def combine_slabs(ys, dest1, dest2, info, residual, rows=MOE_ROWS):
    n = dest1.shape[0]
    nb = n // rows
    cur = pl.BlockSpec((1, 1, rows), lambda i: (i, 0, 0), memory_space=pltpu.SMEM)
    nxt = pl.BlockSpec((1, 1, rows), lambda i: (jnp.minimum(i + 1, nb - 1), 0, 0), memory_space=pltpu.SMEM)
    d1, d2 = dest1.reshape(nb, 1, rows), dest2.reshape(nb, 1, rows)
    return pl.pallas_call(
        functools.partial(_combine_kernel, rows=rows),
        grid=(nb,),
        in_specs=[cur, cur, nxt, nxt, pl.BlockSpec(memory_space=pl.ANY),
                  pl.BlockSpec((rows, LANES), lambda i: (i, 0)),
                  pl.BlockSpec((rows, D_MODEL), lambda i: (i, 0))],
        out_specs=pl.BlockSpec((rows, D_MODEL), lambda i: (i, 0)),
        out_shape=jax.ShapeDtypeStruct((n, D_MODEL), F32),
        scratch_shapes=[pltpu.VMEM((2, rows * SLAB_ROWS, LANES), F32)] * 2 + [pltpu.SemaphoreType.DMA((2, 2))],
        compiler_params=_cparams("arbitrary"),
        name="moe_combine",
    )(d1, d2, d1, d2, ys, info, residual)


def _expert_kernel(be_ref, nu_ref, idx0_ref, idxn_ref, h_ref, wg_ref, wu_ref, wd_ref, o_ref,
                   xbuf, sem, wg_sc, wu_sc, wd_sc, *, rows):
    b = pl.program_id(0)
    n_used = nu_ref[0]
    slot = lax.rem(b, 2)
    prev = be_ref[jnp.maximum(b - 1, 0)]

    @pl.when(b == 0)
    def _():
        _gather_start(h_ref, xbuf.at[0], sem.at[0], idx0_ref, rows)

    @pl.when(b + 1 < n_used)
    def _():
        _gather_start(h_ref, xbuf.at[1 - slot], sem.at[1 - slot], idxn_ref, rows)

    @pl.when((b == 0) | (be_ref[b] != prev))
    def _():
        wg_sc[...] = wg_ref[...].astype(BF16)
        wu_sc[...] = wu_ref[...].astype(BF16)
        wd_sc[...] = wd_ref[...].astype(BF16)

    @pl.when(b < n_used)
    def _():
        _gather_wait(h_ref, xbuf.at[slot], sem.at[slot], rows)
        x = _slab_load(xbuf.at[slot], rows).astype(BF16)
        g = jnp.dot(x, wg_sc[...], preferred_element_type=F32)
        u = jnp.dot(x, wu_sc[...], preferred_element_type=F32)
        hid = _silu(g) * u
        _slab_store(o_ref, jnp.dot(hid.astype(BF16), wd_sc[...], preferred_element_type=F32), rows)

    @pl.when(b >= n_used)
    def _():
        o_ref[...] = jnp.zeros(o_ref.shape, F32)


def routed_experts(h, row_token, block_expert, n_used, w_gate, w_up, w_down, layer, rows=MOE_ROWS):
    n_slots = row_token.shape[0]
    nb = n_slots // rows
    grid_spec = pltpu.PrefetchScalarGridSpec(
        num_scalar_prefetch=2,
        grid=(nb,),
        in_specs=[pl.BlockSpec((1, 1, rows), lambda b, be, nu: (0, 0, 0), memory_space=pltpu.SMEM),
                  pl.BlockSpec((1, 1, rows), lambda b, be, nu: (jnp.minimum(b + 1, nb - 1), 0, 0),
                               memory_space=pltpu.SMEM),
                  pl.BlockSpec(memory_space=pl.ANY),
                  pl.BlockSpec((None, None, D_MODEL, D_EXPERT), lambda b, be, nu: (layer, be[b], 0, 0)),
                  pl.BlockSpec((None, None, D_MODEL, D_EXPERT), lambda b, be, nu: (layer, be[b], 0, 0)),
                  pl.BlockSpec((None, None, D_EXPERT, D_MODEL), lambda b, be, nu: (layer, be[b], 0, 0))],
        out_specs=pl.BlockSpec((rows * SLAB_ROWS, LANES), lambda b, be, nu: (b, 0)),
        scratch_shapes=[pltpu.VMEM((2, rows * SLAB_ROWS, LANES), F32),
                        pltpu.SemaphoreType.DMA((2,)),
                        pltpu.VMEM((D_MODEL, D_EXPERT), BF16),
                        pltpu.VMEM((D_MODEL, D_EXPERT), BF16),
                        pltpu.VMEM((D_EXPERT, D_MODEL), BF16)],
    )
    idx = row_token.reshape(nb, 1, rows)
    return pl.pallas_call(
        functools.partial(_expert_kernel, rows=rows),
        grid_spec=grid_spec,
        out_shape=jax.ShapeDtypeStruct((n_slots * SLAB_ROWS, LANES), F32),
        compiler_params=pltpu.CompilerParams(dimension_semantics=("arbitrary",),
                                             vmem_limit_bytes=EXPERT_VMEM_LIMIT_BYTES),
        name="routed_experts",
    )(block_expert, n_used, idx, idx, h, w_gate, w_up, w_down)


def _routing_plan(e1, e2, rank1, rank2, count, rows):
    n = e1.shape[0]
    n_slots = 2 * n + N_EXPERTS * rows
    padded = (count + rows - 1) // rows * rows
    ends = jnp.cumsum(padded)
    starts = ends - padded
    dest1 = starts[e1] + rank1
    dest2 = starts[e2] + rank2
    tok = jnp.arange(n, dtype=jnp.int32)
    row_token = jnp.zeros((n_slots,), jnp.int32).at[jnp.concatenate([dest1, dest2])].set(
        jnp.concatenate([tok, tok]), unique_indices=True)
    blk_start = jnp.arange(n_slots // rows, dtype=jnp.int32) * rows
    block_expert = jnp.minimum(jnp.sum(ends[None, :] <= blk_start[:, None], axis=1), N_EXPERTS - 1).astype(jnp.int32)
    n_used = (ends[-1] // rows).astype(jnp.int32).reshape(1)
    return row_token, block_expert, n_used, dest1, dest2


def hier_moe_residual(x, norm_w, w_router, b_router, w_gate, w_up, w_down, layer):
    h, info, cnt = moe_router(x, norm_w, w_router, b_router)
    ints = info[:, 0:6].astype(jnp.int32)
    count = cnt[0, N_GROUPS:N_GROUPS + N_EXPERTS].astype(jnp.int32)
    row_token, block_expert, n_used, dest1, dest2 = _routing_plan(
        ints[:, 0], ints[:, 1], ints[:, 4], ints[:, 5], count, MOE_ROWS)
    ys = routed_experts(h, row_token, block_expert, n_used, w_gate, w_up, w_down, layer)
    return combine_slabs(ys, dest1, dest2, info, x)


def _pack_w_in_kernel(w_ref, slab_ref, gates_ref):
    w = w_ref[...]
    o_kr = COL_MISC
    o_cqkv = o_kr + ROPE_B
    o_cb = o_cqkv + QKV_C + H_C * DV_C
    o_g = o_cb + 2 * H_C
    pad = jnp.zeros((w.shape[0], 2 * LANES - ROPE_B - 2 * H_C), w.dtype)
    slab = jnp.concatenate([w[:, :o_cqkv], w[:, o_cb:o_g], pad, w[:, o_cqkv:o_cb]], axis=1)
    slab_ref[...] = slab.astype(BF16)
    gates_ref[...] = w[:, o_g:o_g + N_BRANCH * D_MODEL].astype(BF16)


def _pack_w_in(w, tr=128):
    depth, d, cols = w.shape
    return pl.pallas_call(
        _pack_w_in_kernel,
        grid=(depth, d // tr),
        in_specs=[pl.BlockSpec((None, tr, cols), lambda l, i: (l, i, 0))],
        out_specs=[pl.BlockSpec((None, tr, SLAB_F), lambda l, i: (l, i, 0)),
                   pl.BlockSpec((None, tr, N_BRANCH * D_MODEL), lambda l, i: (l, i, 0))],
        out_shape=[jax.ShapeDtypeStruct((depth, d, SLAB_F), BF16),
                   jax.ShapeDtypeStruct((depth, d, N_BRANCH * D_MODEL), BF16)],
        compiler_params=_cparams("parallel", "parallel"),
        name="pack_w_in",
    )(w)


def _pack_mla(w_uq, w_ukv):
    wq = w_uq.reshape(Q_LORA, H_B, NOPE_B + ROPE_B)
    wq = jnp.pad(wq, ((0, 0), (0, 0), (0, LANES - NOPE_B - ROPE_B))).reshape(Q_LORA, H_B * LANES)
    wkv = w_ukv.reshape(KV_LORA, H_B, NOPE_B + V_B)
    wk = jnp.pad(wkv[:, :, :NOPE_B], ((0, 0), (0, 0), (0, LANES - NOPE_B))).reshape(KV_LORA, H_B * LANES)
    wv = wkv[:, :, NOPE_B:].reshape(KV_LORA, H_B * V_B)
    place = jnp.zeros((ROPE_B, H_B, LANES), F32)
    place = place.at[jnp.arange(ROPE_B), :, NOPE_B + jnp.arange(ROPE_B)].set(1.0)
    return wq.astype(BF16), wk.astype(BF16), place.reshape(ROPE_B, H_B * LANES).astype(BF16), wv.astype(BF16)


def _misc_row(vals, lane0):
    return jnp.zeros((1, LANES), F32).at[0, lane0:lane0 + H_C].set(vals.astype(F32))


def kernel(x_prompt, x_sample, cache_diff_k, cache_diff_v, cache_mla_ckv, cache_mla_krope, state_gdn_conv, state_gdn_s, norm_mix, w_in, diff_lambda, diff_subln, mla_q_norm, mla_w_uq, mla_kv_norm, mla_w_ukv, gdn_conv, gdn_a_log, gdn_dt_bias, gdn_norm, w_branch, w_out, norm_ffn, router_group, router_group_bias, router_expert, router_expert_bias, expert_w_gate, expert_w_up, expert_w_down, norm_final):
    bp, tp, _ = x_prompt.shape
    bs, ts, _ = x_sample.shape
    depth = w_in.shape[0]
    past = cache_diff_k.shape[2]
    n_p, n_s = bp * tp, bs * ts
    tk_s = past + ts

    x = jnp.concatenate([x_prompt.reshape(n_p, D_MODEL), x_sample.reshape(n_s, D_MODEL)], axis=0)
    pos = jnp.concatenate([jnp.tile(jnp.arange(tp), bp), jnp.tile(past + jnp.arange(ts), bs)])
    tabs = (_rope_tables(pos, DH_A, 0, ROT_A // 2)
            + _rope_tables(pos, LANES, NOPE_B, ROPE_B // 2)
            + _rope_tables(pos, LANES, 0, ROPE_B // 2))

    zero_conv = jnp.zeros((bp, CONV_W - 1, QKV_C), F32)
    zero_state = jnp.zeros((bp, H_C, DV_C, DK_C), F32)
    zero_branch = jnp.zeros((n_p + n_s, BRANCH_W), BF16)
    last = np.arange(-(CONV_W - 1), 0)
    tail_p = ((np.arange(bp)[:, None] + 1) * tp + last[None, :]).reshape(-1)
    tail_s = (n_p + (np.arange(bs)[:, None] + 1) * ts + last[None, :]).reshape(-1)
    st_p, st_s = [], []
    w_slab, w_gates = _pack_w_in(jnp.pad(w_in.astype(BF16), ((0, 0), (0, 0), (0, (-w_in.shape[2]) % LANES))))
    w_branch16 = w_branch.astype(BF16)
    w_out16 = w_out.astype(BF16)
    for l in range(depth):
        lam_init = 0.8 - 0.6 * math.exp(-0.3 * l)
        wq, wk, p_kr, wv = _pack_mla(mla_w_uq[l], mla_w_ukv[l])

        h = rmsnorm_rows(x, norm_mix[l], BF16)
        slab = matmul(h, w_slab, l, F32, name="proj_in")
        gates = matmul(h, w_gates, l, BF16, act="sigmoid", name="proj_gates")
        qa, ka, ka_bf, va, va_bf, qb, ckv, kr = mixer_prep(slab, tabs, mla_q_norm[l], wq, mla_kv_norm[l])

        diff_kw = dict(diff=True, lamp=diff_lambda[l], subln=diff_subln[l], lam_init=lam_init)
        oa = attention(qa, ka_bf, va_bf, zero_branch, n_streams=bp, q_row0=0, t_q=tp, t_k=tp, q_off=0,
                       tq=ATTN_TQ, tk=ATTN_TK, name="diff_attn_p", **diff_kw)
        oa = attention_cached(qa, ka_bf, va_bf, cache_diff_k[l].reshape(bs, past, 1024),
                              cache_diff_v[l].reshape(bs, past, 1024), oa, row0=n_p,
                              lamp=diff_lambda[l], subln=diff_subln[l], lam_init=lam_init)

        kb, vb = mla_expand(ckv, kr, wk, p_kr, wv)
        ob = attention(qb, kb, vb, zero_branch, n_streams=bp, q_row0=0, t_q=tp, t_k=tp, q_off=0,
                       tq=ATTN_TQ, tk=ATTN_TK, diff=False, name="mla_attn_p")
        ckv_all = jnp.concatenate([cache_mla_ckv[l], ckv[n_p:].reshape(bs, ts, KV_LORA)], axis=1)
        kr_all = jnp.concatenate([cache_mla_krope[l], kr[n_p:].reshape(bs, ts, ROPE_B)], axis=1)
        kb_s, vb_s = mla_expand(ckv_all.reshape(bs * tk_s, KV_LORA), kr_all.reshape(bs * tk_s, ROPE_B),
                                wk, p_kr, wv)
        ob = attention(qb, kb_s, vb_s, ob, n_streams=bs, q_row0=n_p, t_q=ts, t_k=tk_s, q_off=past,
                       tq=ts, tk=tk_s, diff=False, name="mla_attn_s")

        alog_row = _misc_row(gdn_a_log[l], MISC_A0)
        dtb_row = _misc_row(gdn_dt_bias[l], MISC_A0)
        oc, s_p = gated_delta(slab, zero_conv, zero_state, gdn_conv[l], alog_row, dtb_row, gdn_norm[l], zero_branch,
                              row_blk0=0, n_streams=bp, n_chunks=tp // CHUNK)
        oc, s_s = gated_delta(slab, state_gdn_conv[l], state_gdn_s[l], gdn_conv[l], alog_row, dtb_row,
                              gdn_norm[l], oc, row_blk0=n_p // CHUNK, n_streams=bs, n_chunks=ts // CHUNK)
        conv_p = slab[tail_p][:, COL_CQ:COL_CQ + QKV_C].reshape(bp, CONV_W - 1, QKV_C)
        conv_s = slab[tail_s][:, COL_CQ:COL_CQ + QKV_C].reshape(bs, CONV_W - 1, QKV_C)

        merged = branch_merge(oa, ob, oc, w_branch16, l, gates)
        x = matmul(merged, w_out16, l, F32, residual=x, name="proj_out")

        w_router = jnp.concatenate([router_group[l], router_expert[l],
                                    jnp.zeros((D_MODEL, LANES - N_GROUPS - N_EXPERTS), F32)], axis=1)
        b_router = jnp.concatenate([router_group_bias[l], router_expert_bias[l],
                                    jnp.zeros((LANES - N_GROUPS - N_EXPERTS,), F32)]).reshape(1, LANES)
        x = hier_moe_residual(x, norm_ffn[l], w_router, b_router, expert_w_gate, expert_w_up, expert_w_down, l)

        st_p.append((ka[:n_p].reshape(bp, tp, H_A, 2, DH_A), va[:n_p].reshape(bp, tp, H_A, VA),
                     ckv[:n_p].reshape(bp, tp, KV_LORA), kr[:n_p].reshape(bp, tp, ROPE_B), conv_p, s_p))
        st_s.append((ka[n_p:].reshape(bs, ts, H_A, 2, DH_A), va[n_p:].reshape(bs, ts, H_A, VA),
                     ckv[n_p:].reshape(bs, ts, KV_LORA), kr[n_p:].reshape(bs, ts, ROPE_B), conv_s, s_s))

    y_p, y_s = rmsnorm_split(x, norm_final, n_p)
    y_prompt = y_p.reshape(bp, tp, D_MODEL)
    y_sample = y_s.reshape(bs, ts, D_MODEL)

    def stk(sts, i):
        return jnp.stack([s[i] for s in sts])

    return (y_prompt, y_sample,
            stk(st_p, 0), stk(st_p, 1), stk(st_p, 2), stk(st_p, 3), stk(st_p, 4), stk(st_p, 5),
            stk(st_s, 0), stk(st_s, 1), stk(st_s, 2), stk(st_s, 3), stk(st_s, 4), stk(st_s, 5))
```
